```python
import jax, jax.numpy as jnp
from jax import lax
import numpy as np

D_MODEL = 1024
BATCH = 8
SEQ = 8192
DEPTH = 4

N_A_LAYERS = DEPTH // 2
N_B_LAYERS = DEPTH - N_A_LAYERS
BLK = 128
FOX_HEADS = 16
FOX_HEAD_DIM = D_MODEL // FOX_HEADS
FOX_WIDTH = FOX_HEADS * FOX_HEAD_DIM
FOX_IN = 4 * FOX_WIDTH + FOX_HEADS
DIL_GROUPS = ((128, 1), (512, 4), (2048, 16))
N_GROUPS = len(DIL_GROUPS)
DIL_HEADS = 8
DIL_HEAD_DIM = D_MODEL // DIL_HEADS
DIL_WIDTH = DIL_HEADS * DIL_HEAD_DIM
DIL_IN = (N_GROUPS + 1) * DIL_WIDTH
KV_OUT = 2 * N_GROUPS * DIL_WIDTH
REL_BUCKETS = 32
REL_MAX_DIST = 2048
RMS_EPS = 1e-6
NEG = -1e30

kernel_name = "yoco_fox_dilated_hybrid"


def rmsnorm(x, g):
    x32 = x.astype(jnp.float32)
    y = x32 * lax.rsqrt(jnp.mean(x32 * x32, axis=-1, keepdims=True) + RMS_EPS) * g.astype(jnp.float32)
    return y.astype(x.dtype)


def t5_bucket(dist):
    d = np.asarray(dist)
    exact = REL_BUCKETS // 2
    large = exact + (np.log(np.maximum(d, 1) / exact) / np.log(REL_MAX_DIST / exact)
                     * (REL_BUCKETS - exact)).astype(np.int32)
    large = np.minimum(large, REL_BUCKETS - 1)
    return np.where(d < exact, d, large).astype(np.int32)


def fox_attention(q, k, v, c):
    bsz, seq, nh, hd = q.shape
    nb = seq // BLK
    scale = hd ** -0.5
    q_blocks = (q.astype(jnp.float32) * scale).reshape(bsz, nb, BLK, nh, hd).transpose(1, 0, 2, 3, 4)
    c_t = c.transpose(0, 2, 1)
    c_blocks = c.reshape(bsz, nb, BLK, nh).transpose(1, 0, 3, 2)
    offs = jnp.arange(BLK)

    def one_block(args):
        i, qi, ci = args
        q_pos = i * BLK + offs

        def step(j, carry):
            m, l, acc = carry
            kj = lax.dynamic_slice_in_dim(k, j * BLK, BLK, axis=1).astype(jnp.float32)
            vj = lax.dynamic_slice_in_dim(v, j * BLK, BLK, axis=1).astype(jnp.float32)
            cj = lax.dynamic_slice_in_dim(c_t, j * BLK, BLK, axis=2)
            s = jnp.einsum('bqhd,bkhd->bhqk', qi, kj) + ci[..., :, None] - cj[..., None, :]
            causal = (j * BLK + offs)[None, :] <= q_pos[:, None]
            s = jnp.where(causal, s, NEG)
            m_new = jnp.maximum(m, s.max(-1))
            p = jnp.exp(s - m_new[..., None])
            corr = jnp.exp(m - m_new)
            l = l * corr + p.sum(-1)
            acc = acc * corr[..., None] + jnp.einsum('bhqk,bkhd->bhqd', p, vj)
            return (m_new, l, acc)

        init = (jnp.full((bsz, nh, BLK), NEG, jnp.float32),
                jnp.zeros((bsz, nh, BLK), jnp.float32),
                jnp.zeros((bsz, nh, BLK, hd), jnp.float32))
        m, l, acc = lax.fori_loop(0, i + 1, step, init)
        return acc / l[..., None]

    out = lax.map(one_block, (jnp.arange(nb), q_blocks, c_blocks))
    return out.transpose(1, 0, 3, 2, 4).reshape(bsz, seq, nh, hd)


def dilated_group(q, k, v, window, dilation, table):
    bsz, seq, nh, hd = q.shape
    span = BLK * dilation
    s_pad = -(-seq // span) * span
    sub_len = s_pad // dilation
    nbl = sub_len // BLK
    steps = window // dilation
    scale = hd ** -0.5

    def to_sub(t):
        t = jnp.pad(t.astype(jnp.float32), ((0, 0), (0, s_pad - seq), (0, 0), (0, 0)))
        return t.reshape(bsz, sub_len, dilation, nh, hd).transpose(0, 2, 1, 3, 4)

    def band(t):
        t = jnp.pad(t, ((0, 0), (0, 0), (BLK, 0), (0, 0), (0, 0))).reshape(bsz, dilation, nbl + 1, BLK, nh, hd)
        return jnp.concatenate([t[:, :, :-1], t[:, :, 1:]], axis=3)

    qs = to_sub(q).reshape(bsz, dilation, nbl, BLK, nh, hd) * scale
    kb = band(to_sub(k))
    vb = band(to_sub(v))

    qi = np.arange(BLK)[:, None]
    ki = np.arange(2 * BLK)[None, :]
    rel = qi + BLK - ki
    in_band = (rel >= 0) & (rel <= steps)
    key_idx = np.arange(nbl)[:, None, None] * BLK + ki[None] - BLK
    valid = jnp.asarray(in_band[None] & (key_idx >= 0))[:, None]
    buckets = t5_bucket(np.clip(rel, 0, steps) * dilation)
    bias = jnp.transpose(table.astype(jnp.float32)[buckets], (2, 0, 1))

    s = jnp.einsum('brnqhe,brnkhe->brnhqk', qs, kb) + bias
    s = jnp.where(valid, s, NEG)
    m = s.max(-1)
    p = jnp.exp(s - m[..., None])
    l = p.sum(-1)
    o = jnp.einsum('brnhqk,brnkhe->brnqhe', p, vb) / jnp.moveaxis(l, 3, 4)[..., None]

    def from_sub(t):
        rest = t.shape[4:]
        t = t.reshape((bsz, dilation, sub_len) + rest)
        t = jnp.moveaxis(t, 1, 2).reshape((bsz, s_pad) + rest)
        return t[:, :seq]

    return from_sub(o), from_sub(jnp.moveaxis(m, 3, 4)), from_sub(jnp.moveaxis(l, 3, 4))


def setup_inputs(seed: int = 0) -> dict:
    key = jax.random.key(seed)
    ks = jax.random.split(key, 12)
    f32 = jnp.float32
    return {
        "x": jax.random.normal(ks[0], (BATCH, SEQ, D_MODEL), f32),
        "rel_bias": 0.5 * jax.random.normal(ks[1], (REL_BUCKETS, N_GROUPS * DIL_HEADS), f32),
        "norm_a": 1.0 + 0.02 * jax.random.normal(ks[2], (N_A_LAYERS, D_MODEL), f32),
        "w_in_a": jax.random.normal(ks[3], (N_A_LAYERS, D_MODEL, FOX_IN), f32) * D_MODEL ** -0.5,
        "b_f_a": jax.random.uniform(ks[4], (N_A_LAYERS, FOX_HEADS), f32, minval=1.0, maxval=6.0),
        "w_out_a": jax.random.normal(ks[5], (N_A_LAYERS, FOX_WIDTH, D_MODEL), f32) * FOX_WIDTH ** -0.5,
        "norm_kv": 1.0 + 0.02 * jax.random.normal(ks[6], (D_MODEL,), f32),
        "w_kv": jax.random.normal(ks[7], (D_MODEL, KV_OUT), f32) * D_MODEL ** -0.5,
        "norm_b": 1.0 + 0.02 * jax.random.normal(ks[8], (N_B_LAYERS, D_MODEL), f32),
        "w_in_b": jax.random.normal(ks[9], (N_B_LAYERS, D_MODEL, DIL_IN), f32) * D_MODEL ** -0.5,
        "w_out_b": jax.random.normal(ks[10], (N_B_LAYERS, DIL_WIDTH, D_MODEL), f32) * DIL_WIDTH ** -0.5,
        "norm_f": 1.0 + 0.02 * jax.random.normal(ks[11], (D_MODEL,), f32),
    }


def reference(x, rel_bias, norm_a, w_in_a, b_f_a, w_out_a, norm_kv, w_kv, norm_b, w_in_b, w_out_b, norm_f):
    bsz, seq, _ = x.shape
    h = x
    k_shared = []
    v_shared = []
    for layer in range(DEPTH):
        if layer < N_A_LAYERS:
            i = layer
            u = rmsnorm(h, norm_a[i]) @ w_in_a[i]
            q = u[..., 0 * FOX_WIDTH:1 * FOX_WIDTH].reshape(bsz, seq, FOX_HEADS, FOX_HEAD_DIM)
            k = u[..., 1 * FOX_WIDTH:2 * FOX_WIDTH].reshape(bsz, seq, FOX_HEADS, FOX_HEAD_DIM)
            v = u[..., 2 * FOX_WIDTH:3 * FOX_WIDTH].reshape(bsz, seq, FOX_HEADS, FOX_HEAD_DIM)
            gate = u[..., 3 * FOX_WIDTH:4 * FOX_WIDTH]
            f_logit = u[..., 4 * FOX_WIDTH:].astype(jnp.float32) + b_f_a[i].astype(jnp.float32)
            c = jnp.cumsum(jax.nn.log_sigmoid(f_logit), axis=1)
            o = fox_attention(q, k, v, c).reshape(bsz, seq, FOX_WIDTH).astype(h.dtype)
            h = h + (o * jax.nn.silu(gate)) @ w_out_a[i]
            if layer == N_A_LAYERS - 1:
                kv = (rmsnorm(h, norm_kv) @ w_kv).reshape(bsz, seq, N_GROUPS, 2, DIL_HEADS, DIL_HEAD_DIM)
                k_shared = [kv[:, :, g, 0] for g in range(N_GROUPS)]
                v_shared = [kv[:, :, g, 1] for g in range(N_GROUPS)]
        else:
            i = layer - N_A_LAYERS
            u = rmsnorm(h, norm_b[i]) @ w_in_b[i]
            gate = u[..., N_GROUPS * DIL_WIDTH:]
            outs, ms, ls = [], [], []
            for g, (window, dilation) in enumerate(DIL_GROUPS):
                qg = u[..., g * DIL_WIDTH:(g + 1) * DIL_WIDTH].reshape(bsz, seq, DIL_HEADS, DIL_HEAD_DIM)
                o_g, m_g, l_g = dilated_group(qg, k_shared[g], v_shared[g], window, dilation,
                                              rel_bias[:, g * DIL_HEADS:(g + 1) * DIL_HEADS])
                outs.append(o_g)
                ms.append(m_g)
                ls.append(l_g)
            m_all = jnp.stack(ms)
            wts = jnp.stack(ls) * jnp.exp(m_all - m_all.max(0, keepdims=True))
            o = jnp.einsum('gbsh,gbshe->bshe', wts, jnp.stack(outs)) / wts.sum(0)[..., None]
            o = o.reshape(bsz, seq, DIL_WIDTH).astype(h.dtype)
            h = h + (o * jax.nn.silu(gate)) @ w_out_b[i]
    return rmsnorm(h, norm_f)
```

```python
import functools

import jax
import jax.numpy as jnp
import numpy as np
from jax import lax
from jax.experimental import pallas as pl
from jax.experimental.pallas import tpu as pltpu

D_MODEL = 1024
BLK = 128
FOX_HEADS = 16
FOX_HEAD_DIM = D_MODEL // FOX_HEADS
FOX_WIDTH = FOX_HEADS * FOX_HEAD_DIM
DIL_GROUPS = ((128, 1), (512, 4), (2048, 16))
N_GROUPS = len(DIL_GROUPS)
DIL_HEADS = 8
DIL_HEAD_DIM = D_MODEL // DIL_HEADS
DIL_WIDTH = DIL_HEADS * DIL_HEAD_DIM
REL_BUCKETS = 32
REL_MAX_DIST = 2048
RMS_EPS = 1e-6
NEG = -1e30

LANES = 128
VMEM_LIMIT = 56 * 1024 * 1024
ROW_TILE = 512
COL_CHUNK = 1024
FOX_TQ = 512
DIL_BLOCK = BLK * max(d for _, d in DIL_GROUPS)

F32 = jnp.float32
BF16 = jnp.bfloat16
_NT = (((1,), (1,)), ((), ()))


def _params(*sem):
    return pltpu.CompilerParams(dimension_semantics=sem, vmem_limit_bytes=VMEM_LIMIT)


def _rms(x, g):
    return x * lax.rsqrt(jnp.mean(x * x, axis=-1, keepdims=True) + RMS_EPS) * g


def _split3(x):
    p1 = x.astype(BF16)
    r1 = x - p1.astype(F32)
    p2 = r1.astype(BF16)
    p3 = (r1 - p2.astype(F32)).astype(BF16)
    return p1, p2, p3


def _norm_proj_kernel(*refs, segs, with_forget):
    if with_forget:
        x_ref, g_ref, w_ref, wfh_ref, wfl_ref, bf_ref = refs[:6]
        out_refs = refs[6:6 + len(segs)]
        d_ref = refs[6 + len(segs)]
        carry_ref = refs[7 + len(segs)]
    else:
        x_ref, g_ref, w_ref = refs[:3]
        out_refs = refs[3:3 + len(segs)]

    xn = _rms(x_ref[...], g_ref[...])
    xh = xn.astype(BF16)
    for (start, width, scale), o_ref in zip(segs, out_refs):
        for c in range(0, width, COL_CHUNK):
            cw = min(COL_CHUNK, width - c)
            y = jnp.dot(xh, w_ref[:, start + c:start + c + cw], preferred_element_type=F32)
            if scale != 1.0:
                y = y * scale
            o_ref[:, c:c + cw] = y.astype(o_ref.dtype)

    if with_forget:
        tm = xn.shape[0]

        @pl.when(pl.program_id(1) == 0)
        def _():
            carry_ref[...] = jnp.zeros_like(carry_ref)

        xl = (xn - xh.astype(F32)).astype(BF16)
        wfh = wfh_ref[...]
        f = (lax.dot_general(wfh, xh, _NT, preferred_element_type=F32)
             + lax.dot_general(wfh, xl, _NT, preferred_element_type=F32)
             + lax.dot_general(wfl_ref[...], xh, _NT, preferred_element_type=F32))
        f = f + bf_ref[...]
        ls = jnp.minimum(f, 0.0) - jnp.log1p(jnp.exp(-jnp.abs(f)))
        row = lax.broadcasted_iota(jnp.int32, (tm, tm), 0)
        col = lax.broadcasted_iota(jnp.int32, (tm, tm), 1)
        tri = jnp.where(row <= col, 1.0, 0.0).astype(BF16)
        p1, p2, p3 = _split3(ls)
        c = (jnp.dot(p1, tri, preferred_element_type=F32)
             + jnp.dot(p2, tri, preferred_element_type=F32)
             + jnp.dot(p3, tri, preferred_element_type=F32))
        c = c + carry_ref[:, 0:1]
        carry_ref[...] = jnp.broadcast_to(c[:, tm - 1:tm], carry_ref.shape)
        d_ref[...] = -c


def _norm_proj(h, g, w, segs, forget=None):
    bsz, seq, dm = h.shape
    tm = ROW_TILE
    n_in = w.shape[1]
    kern_segs = tuple((s, wd, sc) for s, wd, sc, _ in segs)
    in_specs = [
        pl.BlockSpec((None, tm, dm), lambda b, i: (b, i, 0)),
        pl.BlockSpec((1, dm), lambda b, i: (0, 0)),
        pl.BlockSpec((dm, n_in), lambda b, i: (0, 0)),
    ]
    args = [h, g.reshape(1, dm).astype(F32), w.astype(BF16)]
    out_shape = [jax.ShapeDtypeStruct((bsz, seq, wd), dt) for _, wd, _, dt in segs]
    out_specs = [pl.BlockSpec((None, tm, wd), lambda b, i: (b, i, 0)) for _, wd, _, _ in segs]
    scratch = []
    if forget is not None:
        w_f, b_f = forget
        nh = w_f.shape[1]
        wft = w_f.T.astype(F32)
        wfh = wft.astype(BF16)
        wfl = (wft - wfh.astype(F32)).astype(BF16)
        in_specs += [
            pl.BlockSpec((nh, dm), lambda b, i: (0, 0)),
            pl.BlockSpec((nh, dm), lambda b, i: (0, 0)),
            pl.BlockSpec((nh, 1), lambda b, i: (0, 0)),
        ]
        args += [wfh, wfl, b_f.reshape(nh, 1).astype(F32)]
        out_shape.append(jax.ShapeDtypeStruct((bsz, nh, seq), F32))
        out_specs.append(pl.BlockSpec((None, nh, tm), lambda b, i: (b, 0, i)))
        scratch.append(pltpu.VMEM((nh, LANES), F32))
    return pl.pallas_call(
        functools.partial(_norm_proj_kernel, segs=kern_segs, with_forget=forget is not None),
        grid=(bsz, seq // tm),
        in_specs=in_specs,
        out_specs=out_specs,
        out_shape=out_shape,
        scratch_shapes=scratch,
        compiler_params=_params("parallel", "arbitrary"),
        name="norm_proj_forget" if forget is not None else "norm_proj",
    )(*args)


def _fox_kernel(q_ref, k_ref, v_ref, d_ref, o_ref, *, tq):
    qi = pl.program_id(2)
    q = q_ref[...]
    lane = lax.broadcasted_iota(jnp.int32, (tq, LANES), 1)
    row = lax.broadcasted_iota(jnp.int32, (tq, tq), 0)
    col = lax.broadcasted_iota(jnp.int32, (tq, tq), 1)
    causal = col <= row
    outs = []
    for hh in range(2):
        head_lanes = (lane < FOX_HEAD_DIM) if hh == 0 else (lane >= FOX_HEAD_DIM)
        qh = jnp.where(head_lanes, q, jnp.zeros_like(q))

        def step(j, carry, masked):
            m, l, acc = carry
            start = pl.multiple_of(j * tq, tq)
            ks = k_ref[pl.ds(start, tq), :]
            vs = v_ref[pl.ds(start, tq), :]
            s = lax.dot_general(qh, ks, _NT, preferred_element_type=F32)
            s = s + d_ref[hh, pl.ds(j, 1), :]
            if masked:
                s = jnp.where(causal, s, NEG)
            m_new = jnp.maximum(m, jnp.max(s, axis=-1, keepdims=True))
            p = jnp.exp(s - m_new)
            corr = jnp.exp(m - m_new)
            l = l * corr + jnp.sum(p, axis=-1, keepdims=True)
            acc = acc * corr + jnp.dot(p.astype(BF16), vs, preferred_element_type=F32)
            return m_new, l, acc

        init = (jnp.full((tq, 1), NEG, F32), jnp.zeros((tq, 1), F32), jnp.zeros((tq, LANES), F32))
        carry = lax.fori_loop(0, qi, functools.partial(step, masked=False), init)
        m, l, acc = step(qi, carry, masked=True)
        outs.append(acc / l)
    o_ref[...] = jnp.where(lane < FOX_HEAD_DIM, outs[0], outs[1]).astype(o_ref.dtype)


def _fox_attention(q, k, v, dneg):
    bsz, seq, _ = q.shape
    tq = FOX_TQ
    npair = FOX_HEADS // 2
    d5 = dneg.reshape(bsz, npair, 2, seq // tq, tq)
    return pl.pallas_call(
        functools.partial(_fox_kernel, tq=tq),
        grid=(bsz, npair, seq // tq),
        in_specs=[
            pl.BlockSpec((None, tq, LANES), lambda b, p, i: (b, i, p)),
            pl.BlockSpec((None, seq, LANES), lambda b, p, i: (b, 0, p)),
            pl.BlockSpec((None, seq, LANES), lambda b, p, i: (b, 0, p)),
            pl.BlockSpec((None, None, 2, seq // tq, tq), lambda b, p, i: (b, p, 0, 0, 0)),
        ],
        out_specs=pl.BlockSpec((None, tq, LANES), lambda b, p, i: (b, i, p)),
        out_shape=jax.ShapeDtypeStruct((bsz, seq, FOX_WIDTH), BF16),
        compiler_params=_params("parallel", "parallel", "arbitrary"),
        name="fox_attention",
    )(q, k, v, d5)


def _out_proj_kernel(*refs, final):
    if final:
        o_ref, gate_ref, h_ref, w_ref, gf_ref, out_ref = refs
    else:
        o_ref, gate_ref, h_ref, w_ref, out_ref = refs
    g = gate_ref[...].astype(F32)
    a = o_ref[...].astype(F32) * (g * (1.0 / (1.0 + jnp.exp(-g))))
    y = h_ref[...] + jnp.dot(a.astype(BF16), w_ref[...], preferred_element_type=F32)
    if final:
        y = _rms(y, gf_ref[...])
    out_ref[...] = y


def _out_proj(o, gate, h, w, final_gain=None):
    bsz, seq, dm = h.shape
    width = o.shape[-1]
    tm = ROW_TILE
    row_spec = lambda wd: pl.BlockSpec((None, tm, wd), lambda b, i: (b, i, 0))
    in_specs = [row_spec(width), row_spec(width), row_spec(dm),
                pl.BlockSpec((width, dm), lambda b, i: (0, 0))]
    args = [o, gate, h, w.astype(BF16)]
    if final_gain is not None:
        in_specs.append(pl.BlockSpec((1, dm), lambda b, i: (0, 0)))
        args.append(final_gain.reshape(1, dm).astype(F32))
    return pl.pallas_call(
        functools.partial(_out_proj_kernel, final=final_gain is not None),
        grid=(bsz, seq // tm),
        in_specs=in_specs,
        out_specs=row_spec(dm),
        out_shape=jax.ShapeDtypeStruct((bsz, seq, dm), F32),
        compiler_params=_params("parallel", "parallel"),
        name="out_proj_final" if final_gain is not None else "out_proj",
    )(*args)


def _t5_bucket(dist):
    d = np.asarray(dist)
    exact = REL_BUCKETS // 2
    large = exact + (np.log(np.maximum(d, 1) / exact) / np.log(REL_MAX_DIST / exact)
                     * (REL_BUCKETS - exact)).astype(np.int32)
    large = np.minimum(large, REL_BUCKETS - 1)
    return np.where(d < exact, d, large).astype(np.int32)


def _bucket_ids():
    qi = np.arange(BLK)[:, None]
    ki = np.arange(2 * BLK)[None, :]
    rel = qi + BLK - ki
    out = []
    for window, dilation in DIL_GROUPS:
        steps = window // dilation
        in_band = (rel >= 0) & (rel <= steps)
        b = _t5_bucket(np.clip(rel, 0, steps) * dilation)
        out.append(np.where(in_band, b, REL_BUCKETS))
    return np.stack(out).astype(np.int32)


def _bias_kernel(table_ref, bucket_ref, out_ref):
    gh = pl.program_id(0)
    bk = bucket_ref[gh // DIL_HEADS]
    acc = jnp.full(bk.shape, NEG, F32)
    for b in range(REL_BUCKETS):
        acc = jnp.where(bk == b, table_ref[b, gh], acc)
    col = lax.broadcasted_iota(jnp.int32, bk.shape, 1)
    out_ref[0] = acc
    out_ref[1] = jnp.where(col < BLK, NEG, acc)


def _bias_tables(rel_bias):
    ngh = N_GROUPS * DIL_HEADS
    return pl.pallas_call(
        _bias_kernel,
        grid=(ngh,),
        in_specs=[
            pl.BlockSpec(memory_space=pltpu.SMEM),
            pl.BlockSpec((N_GROUPS, BLK, 2 * BLK), lambda i: (0, 0, 0)),
        ],
        out_specs=pl.BlockSpec((None, 2, BLK, 2 * BLK), lambda i: (i, 0, 0, 0)),
        out_shape=jax.ShapeDtypeStruct((ngh, 2, BLK, 2 * BLK), F32),
        compiler_params=_params("arbitrary"),
        name="t5_bias_tables",
    )(rel_bias.astype(F32), jnp.asarray(_bucket_ids()))


def _dilated_kernel(*refs):
    q_refs = refs[0:3]
    kv_refs = [refs[3 + 4 * g:7 + 4 * g] for g in range(N_GROUPS)]
    bias_ref = refs[15]
    o_ref = refs[16]
    acc_s, m_s, l_s = refs[17:20]
    first = jnp.where(pl.program_id(2) == 0, 1, 0)

    def rows(ref, start, size, stride):
        if stride == 1:
            return ref[pl.ds(start, size), :]
        return ref[pl.ds(start, size, stride=stride), :]

    for g, (window, dil) in enumerate(DIL_GROUPS):
        span = BLK * dil
        kprev_ref, kcur_ref, vprev_ref, vcur_ref = kv_refs[g]
        for j in range(DIL_BLOCK // span):
            def unit(r, _, g=g, dil=dil, span=span, j=j, kprev_ref=kprev_ref, kcur_ref=kcur_ref,
                     vprev_ref=vprev_ref, vcur_ref=vcur_ref):
                start = j * span + r
                q = rows(q_refs[g], start, BLK, dil).astype(BF16)
                if j == 0:
                    k = jnp.concatenate([rows(kprev_ref, r, BLK, dil), rows(kcur_ref, r, BLK, dil)], axis=0)
                    v = jnp.concatenate([rows(vprev_ref, r, BLK, dil), rows(vcur_ref, r, BLK, dil)], axis=0)
                    bias = bias_ref[g, first]
                else:
                    k = rows(kcur_ref, (j - 1) * span + r, 2 * BLK, dil)
                    v = rows(vcur_ref, (j - 1) * span + r, 2 * BLK, dil)
                    bias = bias_ref[g, 0]
                s = lax.dot_general(q, k.astype(BF16), _NT, preferred_element_type=F32) + bias
                m = jnp.max(s, axis=-1, keepdims=True)
                p = jnp.exp(s - m)
                l = jnp.sum(p, axis=-1, keepdims=True)
                acc = jnp.dot(p.astype(BF16), v.astype(BF16), preferred_element_type=F32)
                if dil == 1:
                    dst = pl.ds(start, BLK)
                else:
                    dst = pl.ds(start, BLK, stride=dil)
                acc_s[g, dst, :] = acc
                m_s[g, dst, :] = jnp.broadcast_to(m, (BLK, LANES))
                l_s[g, dst, :] = jnp.broadcast_to(l, (BLK, LANES))
                return 0

            if dil == 1:
                unit(0, 0)
            else:
                lax.fori_loop(0, dil, unit, 0)

    chunk = 256

    def merge(i, _):
        sl = pl.ds(pl.multiple_of(i * chunk, chunk), chunk)
        ms = [m_s[g, sl, :] for g in range(N_GROUPS)]
        m_all = jnp.maximum(jnp.maximum(ms[0], ms[1]), ms[2])
        num = jnp.zeros((chunk, LANES), F32)
        den = jnp.zeros((chunk, LANES), F32)
        for g in range(N_GROUPS):
            w = jnp.exp(ms[g] - m_all)
            num = num + w * acc_s[g, sl, :]
            den = den + w * l_s[g, sl, :]
        o_ref[sl, :] = (num / den).astype(o_ref.dtype)
        return 0

    lax.fori_loop(0, DIL_BLOCK // chunk, merge, 0)


def _dilated_attention(uq, kv, bias):
    bsz, seq, _ = uq.shape
    nblk = seq // DIL_BLOCK
    in_specs = []
    args = []
    for g in range(N_GROUPS):
        in_specs.append(pl.BlockSpec((None, DIL_BLOCK, LANES),
                                     lambda b, h, n, g=g: (b, n, g * DIL_HEADS + h)))
        args.append(uq)
    for g, (_, dil) in enumerate(DIL_GROUPS):
        span = BLK * dil
        per = DIL_BLOCK // span
        for t in range(2):
            lane_blk = (2 * g + t) * DIL_HEADS
            in_specs.append(pl.BlockSpec(
                (None, span, LANES),
                lambda b, h, n, per=per, lane_blk=lane_blk: (b, jnp.maximum(n * per - 1, 0), lane_blk + h)))
            in_specs.append(pl.BlockSpec(
                (None, DIL_BLOCK, LANES), lambda b, h, n, lane_blk=lane_blk: (b, n, lane_blk + h)))
            args += [kv, kv]
    bias5 = bias.reshape(N_GROUPS, DIL_HEADS, 2, BLK, 2 * BLK)
    in_specs.append(pl.BlockSpec((N_GROUPS, None, 2, BLK, 2 * BLK), lambda b, h, n: (0, h, 0, 0, 0)))
    args.append(bias5)
    state = pltpu.VMEM((N_GROUPS, DIL_BLOCK, LANES), F32)
    return pl.pallas_call(
        _dilated_kernel,
        grid=(bsz, DIL_HEADS, nblk),
        in_specs=in_specs,
        out_specs=pl.BlockSpec((None, DIL_BLOCK, LANES), lambda b, h, n: (b, n, h)),
        out_shape=jax.ShapeDtypeStruct((bsz, seq, DIL_WIDTH), BF16),
        scratch_shapes=[state, state, state],
        compiler_params=_params("parallel", "parallel", "arbitrary"),
        name="dilated_attention",
    )(*args)


def kernel(x, rel_bias, norm_a, w_in_a, b_f_a, w_out_a, norm_kv, w_kv, norm_b, w_in_b, w_out_b, norm_f):
    bsz, seq, dm = x.shape
    assert dm == D_MODEL and seq % DIL_BLOCK == 0 and seq % FOX_TQ == 0 and seq % ROW_TILE == 0
    n_a = w_in_a.shape[0]
    n_b = w_in_b.shape[0]
    w = FOX_WIDTH
    h = x
    for i in range(n_a):
        segs = ((0, w, FOX_HEAD_DIM ** -0.5, BF16), (w, w, 1.0, BF16), (2 * w, w, 1.0, BF16),
                (3 * w, w, 1.0, BF16))
        q, k, v, gate, dneg = _norm_proj(h, norm_a[i], w_in_a[i, :, :4 * w], segs,
                                         forget=(w_in_a[i, :, 4 * w:], b_f_a[i]))
        o = _fox_attention(q, k, v, dneg)
        h = _out_proj(o, gate, h, w_out_a[i])

    (kv,) = _norm_proj(h, norm_kv, w_kv, ((0, w_kv.shape[1], 1.0, F32),))
    bias = _bias_tables(rel_bias)
    gw = N_GROUPS * DIL_WIDTH
    for i in range(n_b):
        segs = ((0, gw, DIL_HEAD_DIM ** -0.5, F32), (gw, DIL_WIDTH, 1.0, BF16))
        uq, gate = _norm_proj(h, norm_b[i], w_in_b[i], segs)
        o = _dilated_attention(uq, kv, bias)
        h = _out_proj(o, gate, h, w_out_b[i], final_gain=norm_f if i == n_b - 1 else None)
    return h
```

```python
import functools

import jax
import jax.numpy as jnp
import numpy as np
from jax import lax
from jax.experimental import pallas as pl
from jax.experimental.pallas import tpu as pltpu

D_MODEL = 1024
BLK = 128
FOX_HEADS = 16
FOX_HEAD_DIM = D_MODEL // FOX_HEADS
FOX_WIDTH = FOX_HEADS * FOX_HEAD_DIM
DIL_GROUPS = ((128, 1), (512, 4), (2048, 16))
N_GROUPS = len(DIL_GROUPS)
DIL_HEADS = 8
DIL_HEAD_DIM = D_MODEL // DIL_HEADS
DIL_WIDTH = DIL_HEADS * DIL_HEAD_DIM
REL_BUCKETS = 32
REL_MAX_DIST = 2048
RMS_EPS = 1e-6
NEG = -1e30
LOG2E = 1.4426950408889634

LANES = 128
VMEM_LIMIT = 56 * 1024 * 1024
ROW_TILE = 512
COL_CHUNK = 1024
FOX_TK = ROW_TILE
FOX_TQ = 512
ONES_ROWS = 16
CAUG_GROUP = FOX_HEADS
DIL_BLOCK = BLK * max(d for _, d in DIL_GROUPS)

F32 = jnp.float32
BF16 = jnp.bfloat16
_NT = (((1,), (1,)), ((), ()))


def _params(*sem):
    return pltpu.CompilerParams(dimension_semantics=sem, vmem_limit_bytes=VMEM_LIMIT)


def _rms(x, g):
    return x * lax.rsqrt(jnp.mean(x * x, axis=-1, keepdims=True) + RMS_EPS) * g


def _split3(x):
    p1 = x.astype(BF16)
    r1 = x - p1.astype(F32)
    p2 = r1.astype(BF16)
    p3 = (r1 - p2.astype(F32)).astype(BF16)
    return p1, p2, p3


def _norm_proj_kernel(*refs, segs):
    x_ref, g_ref, w_ref = refs[:3]
    out_refs = refs[3:3 + len(segs)]
    xh = _rms(x_ref[...], g_ref[...]).astype(BF16)
    for (start, width, scale), o_ref in zip(segs, out_refs):
        for c in range(0, width, COL_CHUNK):
            cw = min(COL_CHUNK, width - c)
            y = jnp.dot(xh, w_ref[:, start + c:start + c + cw], preferred_element_type=F32)
            if scale != 1.0:
                y = y * scale
            o_ref[:, c:c + cw] = y.astype(o_ref.dtype)


def _norm_proj(h, g, w, segs):
    bsz, seq, dm = h.shape
    tm = ROW_TILE
    n_in = w.shape[1]
    kern_segs = tuple((s, wd, sc) for s, wd, sc, _ in segs)
    return pl.pallas_call(
        functools.partial(_norm_proj_kernel, segs=kern_segs),
        grid=(bsz, seq // tm),
        in_specs=[
            pl.BlockSpec((None, tm, dm), lambda b, i: (b, i, 0)),
            pl.BlockSpec((1, dm), lambda b, i: (0, 0)),
            pl.BlockSpec((dm, n_in), lambda b, i: (0, 0)),
        ],
        out_specs=[pl.BlockSpec((None, tm, wd), lambda b, i: (b, i, 0)) for _, wd, _, _ in segs],
        out_shape=[jax.ShapeDtypeStruct((bsz, seq, wd), dt) for _, wd, _, dt in segs],
        compiler_params=_params("parallel", "parallel"),
        name="norm_proj",
    )(h, g.reshape(1, dm).astype(F32), w.astype(BF16))


def _fox_proj_kernel(x_ref, g_ref, wn_ref, wt_ref, wfh_ref, wfl_ref, bf_ref,
                     k_ref, gate_ref, qt_ref, vt_ref, caug_ref, carry_ref):
    tm = x_ref.shape[0]
    w = FOX_WIDTH
    xn = _rms(x_ref[...], g_ref[...])
    xh = xn.astype(BF16)
    k_ref[...] = jnp.dot(xh, wn_ref[:, 0:w], preferred_element_type=F32).astype(BF16)
    gate_ref[...] = jnp.dot(xh, wn_ref[:, w:2 * w], preferred_element_type=F32).astype(BF16)
    rows = 256
    qscale = FOX_HEAD_DIM ** -0.5 * LOG2E
    for r in range(0, w, rows):
        y = lax.dot_general(wt_ref[r:r + rows, :], xh, _NT, preferred_element_type=F32)
        qt_ref[r:r + rows, :] = (y * qscale).astype(BF16)
    for r in range(0, w, rows):
        y = lax.dot_general(wt_ref[w + r:w + r + rows, :], xh, _NT, preferred_element_type=F32)
        vt_ref[r:r + rows, :] = y.astype(BF16)

    @pl.when(pl.program_id(1) == 0)
    def _():
        carry_ref[...] = jnp.zeros_like(carry_ref)

    xl = (xn - xh.astype(F32)).astype(BF16)
    wfh = wfh_ref[...]
    f = (jnp.dot(xh, wfh, preferred_element_type=F32) + jnp.dot(xl, wfh, preferred_element_type=F32)
         + jnp.dot(xh, wfl_ref[...], preferred_element_type=F32)) + bf_ref[...]
    ls = jnp.minimum(f, 0.0) - jnp.log1p(jnp.exp(-jnp.abs(f)))
    row = lax.broadcasted_iota(jnp.int32, (tm, tm), 0)
    col = lax.broadcasted_iota(jnp.int32, (tm, tm), 1)
    tri = jnp.where(row >= col, 1.0, 0.0).astype(BF16)
    p1, p2, p3 = _split3(ls)
    c = (jnp.dot(tri, p1, preferred_element_type=F32) + jnp.dot(tri, p2, preferred_element_type=F32)
         + jnp.dot(tri, p3, preferred_element_type=F32)) + carry_ref[0:1, :]
    carry_ref[...] = jnp.broadcast_to(c[tm - 1:tm, :], carry_ref.shape)
    d1, d2, d3 = _split3(c * (-LOG2E))
    lane = lax.broadcasted_iota(jnp.int32, (tm, LANES), 1)
    caug_ref[...] = jnp.where(lane < CAUG_GROUP, d1,
                              jnp.where(lane < 2 * CAUG_GROUP, d2,
                                        jnp.where(lane < 3 * CAUG_GROUP, d3, jnp.zeros_like(d1))))


def _fox_proj(h, g, w_in, b_f):
    bsz, seq, dm = h.shape
    tm = ROW_TILE
    w = FOX_WIDTH
    wn = jnp.concatenate([w_in[:, w:2 * w], w_in[:, 3 * w:4 * w]], axis=1).astype(BF16)
    wt = jnp.concatenate([w_in[:, 0:w], w_in[:, 2 * w:3 * w]], axis=1).T.astype(BF16)
    w_f = w_in[:, 4 * w:].astype(F32)
    pad = jnp.zeros((dm, LANES - 3 * FOX_HEADS), F32)
    wf_rep = jnp.concatenate([w_f, w_f, w_f, pad], axis=1)
    wfh = wf_rep.astype(BF16)
    wfl = (wf_rep - wfh.astype(F32)).astype(BF16)
    bf = b_f.astype(F32)
    bf_rep = jnp.concatenate([bf, bf, bf, jnp.zeros((LANES - 3 * FOX_HEADS,), F32)]).reshape(1, LANES)
    const = lambda shape: pl.BlockSpec(shape, lambda b, i: (0,) * len(shape))
    return pl.pallas_call(
        _fox_proj_kernel,
        grid=(bsz, seq // tm),
        in_specs=[
            pl.BlockSpec((None, tm, dm), lambda b, i: (b, i, 0)),
            const((1, dm)), const((dm, 2 * w)), const((2 * w, dm)),
            const((dm, LANES)), const((dm, LANES)), const((1, LANES)),
        ],
        out_specs=[
            pl.BlockSpec((None, tm, w), lambda b, i: (b, i, 0)),
            pl.BlockSpec((None, tm, w), lambda b, i: (b, i, 0)),
            pl.BlockSpec((None, w, tm), lambda b, i: (b, 0, i)),
            pl.BlockSpec((None, None, w, tm), lambda b, i: (b, i, 0, 0)),
            pl.BlockSpec((None, tm, LANES), lambda b, i: (b, i, 0)),
        ],
        out_shape=[
            jax.ShapeDtypeStruct((bsz, seq, w), BF16),
            jax.ShapeDtypeStruct((bsz, seq, w), BF16),
            jax.ShapeDtypeStruct((bsz, w, seq), BF16),
            jax.ShapeDtypeStruct((bsz, seq // tm, w, tm), BF16),
            jax.ShapeDtypeStruct((bsz, seq, LANES), BF16),
        ],
        scratch_shapes=[pltpu.VMEM((8, LANES), F32)],
        compiler_params=_params("parallel", "arbitrary"),
        name="fox_proj",
    )(h, g.reshape(1, dm).astype(F32), wn, wt, wfh, wfl, bf_rep)


def _fox_kernel(qt_ref, k_ref, c_ref, vt_ref, o_ref, st_s, p_s, *, tq, tk):
    pair = pl.program_id(1)
    qi = pl.program_id(2)
    hd = FOX_HEAD_DIM
    qt = qt_ref[...]
    row = lax.broadcasted_iota(jnp.int32, (LANES, tq), 0)
    wq = []
    for hh in range(2):
        head = 2 * pair + hh
        qh = jnp.where((row < hd) if hh == 0 else (row >= hd), qt, jnp.zeros_like(qt))
        aug = jnp.where(((row & (CAUG_GROUP - 1)) == head) & (row < 3 * CAUG_GROUP), 1.0, 0.0)
        wq.append(jnp.concatenate([qh, aug.astype(BF16)], axis=0))
    ones = jnp.ones((ONES_ROWS, tk), BF16)
    krow = lax.broadcasted_iota(jnp.int32, (tk, tq), 0)
    qcol = lax.broadcasted_iota(jnp.int32, (tk, tq), 1)

    def scores(j):
        start = pl.multiple_of(j * tk, tk)
        lhs = jnp.concatenate([k_ref[pl.ds(start, tk), :], c_ref[pl.ds(start, tk), :]], axis=1)
        for hh in range(2):
            st_s[hh] = jnp.dot(lhs, wq[hh], preferred_element_type=F32)

    def softmax(j, m, masked):
        corrs, ms = [], []
        for hh in range(2):
            s = st_s[hh]
            if masked:
                s = jnp.where(krow - qcol <= qi * tq - j * tk, s, NEG)
            m_new = jnp.maximum(m[hh], jnp.max(s, axis=0, keepdims=True))
            p_s[hh] = jnp.exp2(s - m_new).astype(BF16)
            corrs.append(jnp.exp2(m[hh] - m_new))
            ms.append(m_new)
        return tuple(corrs), tuple(ms)

    def pv(j, corr, acc):
        out = []
        for hh in range(2):
            vv = jnp.concatenate([vt_ref[j, hh * hd:(hh + 1) * hd, :], ones], axis=0)
            out.append(acc[hh] * corr[hh] + jnp.dot(vv, p_s[hh], preferred_element_type=F32))
        return tuple(out)

    def body(j, carry):
        corr, m, acc = carry
        acc = pv(jnp.maximum(j - 1, 0), corr, acc)
        corr, m = softmax(j, m, masked=False)
        scores(j + 1)
        return corr, m, acc

    nfull = (qi * tq) // tk
    two = lambda v: (v, v)
    p_s[...] = jnp.zeros_like(p_s)
    scores(0)
    init = (two(jnp.ones((1, tq), F32)), two(jnp.full((1, tq), NEG, F32)),
            two(jnp.zeros((hd + ONES_ROWS, tq), F32)))
    corr, m, acc = lax.fori_loop(0, nfull, body, init)
    acc = pv(jnp.maximum(nfull - 1, 0), corr, acc)
    corr, m = softmax(nfull, m, masked=True)
    acc = pv(nfull, corr, acc)
    ot = jnp.concatenate([a[0:hd] / a[hd:hd + 1] for a in acc], axis=0)
    o_ref[...] = ot.T.astype(o_ref.dtype)


def _fox_attention(qt, k, vt, caug):
    bsz, seq, _ = k.shape
    tq, tk = FOX_TQ, FOX_TK
    npair = FOX_HEADS // 2
    return pl.pallas_call(
        functools.partial(_fox_kernel, tq=tq, tk=tk),
        grid=(bsz, npair, seq // tq),
        in_specs=[
            pl.BlockSpec((None, LANES, tq), lambda b, p, i: (b, p, i)),
            pl.BlockSpec((None, seq, LANES), lambda b, p, i: (b, 0, p)),
            pl.BlockSpec((None, seq, LANES), lambda b, p, i: (b, 0, 0)),
            pl.BlockSpec((None, seq // tk, LANES, tk), lambda b, p, i: (b, 0, p, 0)),
        ],
        out_specs=pl.BlockSpec((None, tq, LANES), lambda b, p, i: (b, i, p)),
        out_shape=jax.ShapeDtypeStruct((bsz, seq, FOX_WIDTH), BF16),
        scratch_shapes=[pltpu.VMEM((2, tk, tq), F32), pltpu.VMEM((2, tk, tq), BF16)],
        compiler_params=_params("parallel", "parallel", "arbitrary"),
        name="fox_attention",
    )(qt, k, caug, vt)


def _out_proj_kernel(*refs, final):
    if final:
        o_ref, gate_ref, h_ref, w_ref, gf_ref, out_ref = refs
    else:
        o_ref, gate_ref, h_ref, w_ref, out_ref = refs
    g = gate_ref[...].astype(F32)
    a = o_ref[...].astype(F32) * (g * (1.0 / (1.0 + jnp.exp(-g))))
    y = h_ref[...] + jnp.dot(a.astype(BF16), w_ref[...], preferred_element_type=F32)
    if final:
        y = _rms(y, gf_ref[...])
    out_ref[...] = y


def _out_proj(o, gate, h, w, final_gain=None):
    bsz, seq, dm = h.shape
    width = o.shape[-1]
    tm = ROW_TILE
    row_spec = lambda wd: pl.BlockSpec((None, tm, wd), lambda b, i: (b, i, 0))
    in_specs = [row_spec(width), row_spec(width), row_spec(dm),
                pl.BlockSpec((width, dm), lambda b, i: (0, 0))]
    args = [o, gate, h, w.astype(BF16)]
    if final_gain is not None:
        in_specs.append(pl.BlockSpec((1, dm), lambda b, i: (0, 0)))
        args.append(final_gain.reshape(1, dm).astype(F32))
    return pl.pallas_call(
        functools.partial(_out_proj_kernel, final=final_gain is not None),
        grid=(bsz, seq // tm),
        in_specs=in_specs,
        out_specs=row_spec(dm),
        out_shape=jax.ShapeDtypeStruct((bsz, seq, dm), F32),
        compiler_params=_params("parallel", "parallel"),
        name="out_proj_final" if final_gain is not None else "out_proj",
    )(*args)


def _t5_bucket(dist):
    d = np.asarray(dist)
    exact = REL_BUCKETS // 2
    large = exact + (np.log(np.maximum(d, 1) / exact) / np.log(REL_MAX_DIST / exact)
                     * (REL_BUCKETS - exact)).astype(np.int32)
    large = np.minimum(large, REL_BUCKETS - 1)
    return np.where(d < exact, d, large).astype(np.int32)


def _bucket_ids():
    qi = np.arange(BLK)[:, None]
    ki = np.arange(2 * BLK)[None, :]
    rel = qi + BLK - ki
    out = []
    for window, dilation in DIL_GROUPS:
        steps = window // dilation
        in_band = (rel >= 0) & (rel <= steps)
        b = _t5_bucket(np.clip(rel, 0, steps) * dilation)
        out.append(np.where(in_band, b, REL_BUCKETS))
    return np.stack(out).astype(np.int32)


def _bias_kernel(table_ref, bucket_ref, out_ref):
    gh = pl.program_id(0)
    bk = bucket_ref[gh // DIL_HEADS]
    acc = jnp.full(bk.shape, NEG, F32)
    for b in range(REL_BUCKETS):
        acc = jnp.where(bk == b, table_ref[b, gh], acc)
    col = lax.broadcasted_iota(jnp.int32, bk.shape, 1)
    out_ref[0] = acc
    out_ref[1] = jnp.where(col < BLK, NEG, acc)


def _bias_tables(rel_bias):
    ngh = N_GROUPS * DIL_HEADS
    return pl.pallas_call(
        _bias_kernel,
        grid=(ngh,),
        in_specs=[
            pl.BlockSpec(memory_space=pltpu.SMEM),
            pl.BlockSpec((N_GROUPS, BLK, 2 * BLK), lambda i: (0, 0, 0)),
        ],
        out_specs=pl.BlockSpec((None, 2, BLK, 2 * BLK), lambda i: (i, 0, 0, 0)),
        out_shape=jax.ShapeDtypeStruct((ngh, 2, BLK, 2 * BLK), F32),
        compiler_params=_params("arbitrary"),
        name="t5_bias_tables",
    )(rel_bias.astype(F32), jnp.asarray(_bucket_ids()))


def _dilated_kernel(*refs):
    q_refs = refs[0:3]
    kv_refs = [refs[3 + 4 * g:7 + 4 * g] for g in range(N_GROUPS)]
    bias_ref = refs[15]
    o_ref = refs[16]
    acc_s, m_s, l_s = refs[17:20]
    first = jnp.where(pl.program_id(2) == 0, 1, 0)

    def rows(ref, start, size, stride):
        if stride == 1:
            return ref[pl.ds(start, size), :]
        return ref[pl.ds(start, size, stride=stride), :]

    for g, (window, dil) in enumerate(DIL_GROUPS):
        span = BLK * dil
        kprev_ref, kcur_ref, vprev_ref, vcur_ref = kv_refs[g]
        for j in range(DIL_BLOCK // span):
            def unit(r, _, g=g, dil=dil, span=span, j=j, kprev_ref=kprev_ref, kcur_ref=kcur_ref,
                     vprev_ref=vprev_ref, vcur_ref=vcur_ref):
                start = j * span + r
                q = rows(q_refs[g], start, BLK, dil).astype(BF16)
                if j == 0:
                    k = jnp.concatenate([rows(kprev_ref, r, BLK, dil), rows(kcur_ref, r, BLK, dil)], axis=0)
                    v = jnp.concatenate([rows(vprev_ref, r, BLK, dil), rows(vcur_ref, r, BLK, dil)], axis=0)
                    bias = bias_ref[g, first]
                else:
                    k = rows(kcur_ref, (j - 1) * span + r, 2 * BLK, dil)
                    v = rows(vcur_ref, (j - 1) * span + r, 2 * BLK, dil)
                    bias = bias_ref[g, 0]
                s = lax.dot_general(q, k.astype(BF16), _NT, preferred_element_type=F32) + bias
                m = jnp.max(s, axis=-1, keepdims=True)
                p = jnp.exp(s - m)
                l = jnp.sum(p, axis=-1, keepdims=True)
                acc = jnp.dot(p.astype(BF16), v.astype(BF16), preferred_element_type=F32)
                if dil == 1:
                    dst = pl.ds(start, BLK)
                else:
                    dst = pl.ds(start, BLK, stride=dil)
                acc_s[g, dst, :] = acc
                m_s[g, dst, :] = jnp.broadcast_to(m, (BLK, LANES))
                l_s[g, dst, :] = jnp.broadcast_to(l, (BLK, LANES))
                return 0

            if dil == 1:
                unit(0, 0)
            else:
                lax.fori_loop(0, dil, unit, 0)

    chunk = 256

    def merge(i, _):
        sl = pl.ds(pl.multiple_of(i * chunk, chunk), chunk)
        ms = [m_s[g, sl, :] for g in range(N_GROUPS)]
        m_all = jnp.maximum(jnp.maximum(ms[0], ms[1]), ms[2])
        num = jnp.zeros((chunk, LANES), F32)
        den = jnp.zeros((chunk, LANES), F32)
        for g in range(N_GROUPS):
            w = jnp.exp(ms[g] - m_all)
            num = num + w * acc_s[g, sl, :]
            den = den + w * l_s[g, sl, :]
        o_ref[sl, :] = (num / den).astype(o_ref.dtype)
        return 0

    lax.fori_loop(0, DIL_BLOCK // chunk, merge, 0)


def _dilated_attention(uq, kv, bias):
    bsz, seq, _ = uq.shape
    nblk = seq // DIL_BLOCK
    in_specs = []
    args = []
    for g in range(N_GROUPS):
        in_specs.append(pl.BlockSpec((None, DIL_BLOCK, LANES),
                                     lambda b, h, n, g=g: (b, n, g * DIL_HEADS + h)))
        args.append(uq)
    for g, (_, dil) in enumerate(DIL_GROUPS):
        span = BLK * dil
        per = DIL_BLOCK // span
        for t in range(2):
            lane_blk = (2 * g + t) * DIL_HEADS
            in_specs.append(pl.BlockSpec(
                (None, span, LANES),
                lambda b, h, n, per=per, lane_blk=lane_blk: (b, jnp.maximum(n * per - 1, 0), lane_blk + h)))
            in_specs.append(pl.BlockSpec(
                (None, DIL_BLOCK, LANES), lambda b, h, n, lane_blk=lane_blk: (b, n, lane_blk + h)))
            args += [kv, kv]
    bias5 = bias.reshape(N_GROUPS, DIL_HEADS, 2, BLK, 2 * BLK)
    in_specs.append(pl.BlockSpec((N_GROUPS, None, 2, BLK, 2 * BLK), lambda b, h, n: (0, h, 0, 0, 0)))
    args.append(bias5)
    state = pltpu.VMEM((N_GROUPS, DIL_BLOCK, LANES), F32)
    return pl.pallas_call(
        _dilated_kernel,
        grid=(bsz, DIL_HEADS, nblk),
        in_specs=in_specs,
        out_specs=pl.BlockSpec((None, DIL_BLOCK, LANES), lambda b, h, n: (b, n, h)),
        out_shape=jax.ShapeDtypeStruct((bsz, seq, DIL_WIDTH), BF16),
        scratch_shapes=[state, state, state],
        compiler_params=_params("parallel", "parallel", "arbitrary"),
        name="dilated_attention",
    )(*args)


def kernel(x, rel_bias, norm_a, w_in_a, b_f_a, w_out_a, norm_kv, w_kv, norm_b, w_in_b, w_out_b, norm_f):
    bsz, seq, dm = x.shape
    assert dm == D_MODEL and seq % DIL_BLOCK == 0 and seq % ROW_TILE == 0
    n_a = w_in_a.shape[0]
    n_b = w_in_b.shape[0]
    h = x
    for i in range(n_a):
        k, gate, qt, vt, caug = _fox_proj(h, norm_a[i], w_in_a[i], b_f_a[i])
        o = _fox_attention(qt, k, vt, caug)
        h = _out_proj(o, gate, h, w_out_a[i])

    (kv,) = _norm_proj(h, norm_kv, w_kv, ((0, w_kv.shape[1], 1.0, F32),))
    bias = _bias_tables(rel_bias)
    gw = N_GROUPS * DIL_WIDTH
    for i in range(n_b):
        segs = ((0, gw, DIL_HEAD_DIM ** -0.5, F32), (gw, DIL_WIDTH, 1.0, BF16))
        uq, gate = _norm_proj(h, norm_b[i], w_in_b[i], segs)
        o = _dilated_attention(uq, kv, bias)
        h = _out_proj(o, gate, h, w_out_b[i], final_gain=norm_f if i == n_b - 1 else None)
    return h
```

```python
import functools

import jax
import jax.numpy as jnp
import numpy as np
from jax import lax
from jax.experimental import pallas as pl
from jax.experimental.pallas import tpu as pltpu

D_MODEL = 1024
BLK = 128
FOX_HEADS = 16
FOX_HEAD_DIM = D_MODEL // FOX_HEADS
FOX_WIDTH = FOX_HEADS * FOX_HEAD_DIM
DIL_GROUPS = ((128, 1), (512, 4), (2048, 16))
N_GROUPS = len(DIL_GROUPS)
DIL_HEADS = 8
DIL_HEAD_DIM = D_MODEL // DIL_HEADS
DIL_WIDTH = DIL_HEADS * DIL_HEAD_DIM
REL_BUCKETS = 32
REL_MAX_DIST = 2048
RMS_EPS = 1e-6
NEG = -1e30
LOG2E = 1.4426950408889634

LANES = 128
VMEM_LIMIT = 56 * 1024 * 1024
ROW_TILE = 512
COL_CHUNK = 1024
FOX_TK = ROW_TILE
FOX_TQ = 512
FOX_PAIRS = 2
ONES_ROWS = 16
CAUG_GROUP = FOX_HEADS
DIL_BLOCK = BLK * max(d for _, d in DIL_GROUPS)

F32 = jnp.float32
BF16 = jnp.bfloat16
_NT = (((1,), (1,)), ((), ()))


def _params(*sem):
    return pltpu.CompilerParams(dimension_semantics=sem, vmem_limit_bytes=VMEM_LIMIT)


def _rms(x, g):
    return x * lax.rsqrt(jnp.mean(x * x, axis=-1, keepdims=True) + RMS_EPS) * g


def _split3(x):
    p1 = x.astype(BF16)
    r1 = x - p1.astype(F32)
    p2 = r1.astype(BF16)
    p3 = (r1 - p2.astype(F32)).astype(BF16)
    return p1, p2, p3


def _norm_proj_kernel(*refs, segs):
    x_ref, g_ref, w_ref = refs[:3]
    out_refs = refs[3:3 + len(segs)]
    xh = _rms(x_ref[...], g_ref[...]).astype(BF16)
    for (start, width, scale), o_ref in zip(segs, out_refs):
        for c in range(0, width, COL_CHUNK):
            cw = min(COL_CHUNK, width - c)
            y = jnp.dot(xh, w_ref[:, start + c:start + c + cw], preferred_element_type=F32)
            if scale != 1.0:
                y = y * scale
            o_ref[:, c:c + cw] = y.astype(o_ref.dtype)


def _norm_proj(h, g, w, segs):
    bsz, seq, dm = h.shape
    tm = ROW_TILE
    n_in = w.shape[1]
    kern_segs = tuple((s, wd, sc) for s, wd, sc, _ in segs)
    return pl.pallas_call(
        functools.partial(_norm_proj_kernel, segs=kern_segs),
        grid=(bsz, seq // tm),
        in_specs=[
            pl.BlockSpec((None, tm, dm), lambda b, i: (b, i, 0)),
            pl.BlockSpec((1, dm), lambda b, i: (0, 0)),
            pl.BlockSpec((dm, n_in), lambda b, i: (0, 0)),
        ],
        out_specs=[pl.BlockSpec((None, tm, wd), lambda b, i: (b, i, 0)) for _, wd, _, _ in segs],
        out_shape=[jax.ShapeDtypeStruct((bsz, seq, wd), dt) for _, wd, _, dt in segs],
        compiler_params=_params("parallel", "parallel"),
        name="norm_proj",
    )(h, g.reshape(1, dm).astype(F32), w.astype(BF16))


def _fox_proj_kernel(x_ref, g_ref, wn_ref, wt_ref, wfh_ref, wfl_ref, bf_ref,
                     k_ref, gate_ref, qt_ref, vt_ref, caug_ref, carry_ref):
    tm = x_ref.shape[0]
    w = FOX_WIDTH
    xn = _rms(x_ref[...], g_ref[...])
    xh = xn.astype(BF16)
    k_ref[...] = jnp.dot(xh, wn_ref[:, 0:w], preferred_element_type=F32).astype(BF16)
    gate_ref[...] = jnp.dot(xh, wn_ref[:, w:2 * w], preferred_element_type=F32).astype(BF16)
    rows = 256
    qscale = FOX_HEAD_DIM ** -0.5 * LOG2E
    for r in range(0, w, rows):
        y = lax.dot_general(wt_ref[r:r + rows, :], xh, _NT, preferred_element_type=F32)
        qt_ref[r:r + rows, :] = (y * qscale).astype(BF16)
    for r in range(0, w, rows):
        y = lax.dot_general(wt_ref[w + r:w + r + rows, :], xh, _NT, preferred_element_type=F32)
        vt_ref[r:r + rows, :] = y.astype(BF16)

    @pl.when(pl.program_id(1) == 0)
    def _():
        carry_ref[...] = jnp.zeros_like(carry_ref)

    xl = (xn - xh.astype(F32)).astype(BF16)
    wfh = wfh_ref[...]
    f = (jnp.dot(xh, wfh, preferred_element_type=F32) + jnp.dot(xl, wfh, preferred_element_type=F32)
         + jnp.dot(xh, wfl_ref[...], preferred_element_type=F32)) + bf_ref[...]
    ls = jnp.minimum(f, 0.0) - jnp.log1p(jnp.exp(-jnp.abs(f)))
    row = lax.broadcasted_iota(jnp.int32, (tm, tm), 0)
    col = lax.broadcasted_iota(jnp.int32, (tm, tm), 1)
    tri = jnp.where(row >= col, 1.0, 0.0).astype(BF16)
    p1, p2, p3 = _split3(ls)
    c = (jnp.dot(tri, p1, preferred_element_type=F32) + jnp.dot(tri, p2, preferred_element_type=F32)
         + jnp.dot(tri, p3, preferred_element_type=F32)) + carry_ref[0:1, :]
    carry_ref[...] = jnp.broadcast_to(c[tm - 1:tm, :], carry_ref.shape)
    d1, d2, d3 = _split3(c * (-LOG2E))
    lane = lax.broadcasted_iota(jnp.int32, (tm, LANES), 1)
    caug_ref[...] = jnp.where(lane < CAUG_GROUP, d1,
                              jnp.where(lane < 2 * CAUG_GROUP, d2,
                                        jnp.where(lane < 3 * CAUG_GROUP, d3, jnp.zeros_like(d1))))


def _fox_proj(h, g, w_in, b_f):
    bsz, seq, dm = h.shape
    tm = ROW_TILE
    w = FOX_WIDTH
    wn = jnp.concatenate([w_in[:, w:2 * w], w_in[:, 3 * w:4 * w]], axis=1).astype(BF16)
    wt = jnp.concatenate([w_in[:, 0:w], w_in[:, 2 * w:3 * w]], axis=1).T.astype(BF16)
    w_f = w_in[:, 4 * w:].astype(F32)
    pad = jnp.zeros((dm, LANES - 3 * FOX_HEADS), F32)
    wf_rep = jnp.concatenate([w_f, w_f, w_f, pad], axis=1)
    wfh = wf_rep.astype(BF16)
    wfl = (wf_rep - wfh.astype(F32)).astype(BF16)
    bf = b_f.astype(F32)
    bf_rep = jnp.concatenate([bf, bf, bf, jnp.zeros((LANES - 3 * FOX_HEADS,), F32)]).reshape(1, LANES)
    const = lambda shape: pl.BlockSpec(shape, lambda b, i: (0,) * len(shape))
    return pl.pallas_call(
        _fox_proj_kernel,
        grid=(bsz, seq // tm),
        in_specs=[
            pl.BlockSpec((None, tm, dm), lambda b, i: (b, i, 0)),
            const((1, dm)), const((dm, 2 * w)), const((2 * w, dm)),
            const((dm, LANES)), const((dm, LANES)), const((1, LANES)),
        ],
        out_specs=[
            pl.BlockSpec((None, tm, w), lambda b, i: (b, i, 0)),
            pl.BlockSpec((None, tm, w), lambda b, i: (b, i, 0)),
            pl.BlockSpec((None, w, tm), lambda b, i: (b, 0, i)),
            pl.BlockSpec((None, None, w, tm), lambda b, i: (b, i, 0, 0)),
            pl.BlockSpec((None, tm, LANES), lambda b, i: (b, i, 0)),
        ],
        out_shape=[
            jax.ShapeDtypeStruct((bsz, seq, w), BF16),
            jax.ShapeDtypeStruct((bsz, seq, w), BF16),
            jax.ShapeDtypeStruct((bsz, w, seq), BF16),
            jax.ShapeDtypeStruct((bsz, seq // tm, w, tm), BF16),
            jax.ShapeDtypeStruct((bsz, seq, LANES), BF16),
        ],
        scratch_shapes=[pltpu.VMEM((8, LANES), F32)],
        compiler_params=_params("parallel", "arbitrary"),
        name="fox_proj",
    )(h, g.reshape(1, dm).astype(F32), wn, wt, wfh, wfl, bf_rep)


def _fox_kernel(qt_ref, k_ref, c_ref, vt_ref, o_ref, st_s, p_s, *, tq, tk):
    grp = pl.program_id(1)
    qi = pl.program_id(2)
    hd = FOX_HEAD_DIM
    nh = 2 * FOX_PAIRS
    row = lax.broadcasted_iota(jnp.int32, (LANES, tq), 0)
    wq = []
    for i in range(nh):
        head = nh * grp + i
        qt = qt_ref[(i // 2) * LANES:(i // 2 + 1) * LANES, :]
        qh = jnp.where((row < hd) if i % 2 == 0 else (row >= hd), qt, jnp.zeros_like(qt))
        aug = jnp.where(((row & (CAUG_GROUP - 1)) == head) & (row < 3 * CAUG_GROUP), 1.0, 0.0)
        wq.append(jnp.concatenate([qh, aug.astype(BF16)], axis=0))
    ones = jnp.ones((ONES_ROWS, tk), BF16)
    krow = lax.broadcasted_iota(jnp.int32, (tk, tq), 0)
    qcol = lax.broadcasted_iota(jnp.int32, (tk, tq), 1)

    def scores(j):
        start = pl.multiple_of(j * tk, tk)
        c = c_ref[pl.ds(start, tk), :]
        lhs = [jnp.concatenate([k_ref[pl.ds(start, tk), pp * LANES:(pp + 1) * LANES], c], axis=1)
               for pp in range(FOX_PAIRS)]
        bmax = []
        for i in range(nh):
            s = jnp.dot(lhs[i // 2], wq[i], preferred_element_type=F32)
            st_s[i] = s
            bmax.append(jnp.max(s, axis=0, keepdims=True))
        return tuple(bmax)

    def softmax(j, m, bmax, masked):
        corrs, ms = [], []
        for i in range(nh):
            s = st_s[i]
            if masked:
                s = jnp.where(krow - qcol <= qi * tq - j * tk, s, NEG)
                m_new = jnp.maximum(m[i], jnp.max(s, axis=0, keepdims=True))
            else:
                m_new = jnp.maximum(m[i], bmax[i])
            p_s[i] = jnp.exp2(s - m_new).astype(BF16)
            corrs.append(jnp.exp2(m[i] - m_new))
            ms.append(m_new)
        return tuple(corrs), tuple(ms)

    def pv(j, corr, acc):
        out = []
        for i in range(nh):
            vv = jnp.concatenate([vt_ref[j, i * hd:(i + 1) * hd, :], ones], axis=0)
            out.append(acc[i] * corr[i] + jnp.dot(vv, p_s[i], preferred_element_type=F32))
        return tuple(out)

    def body(j, carry):
        corr, m, bmax, acc = carry
        acc = pv(jnp.maximum(j - 1, 0), corr, acc)
        corr, m = softmax(j, m, bmax, masked=False)
        bmax = scores(j + 1)
        return corr, m, bmax, acc

    nfull = (qi * tq) // tk
    rep = lambda v: (v,) * nh
    p_s[...] = jnp.zeros_like(p_s)
    init = (rep(jnp.ones((1, tq), F32)), rep(jnp.full((1, tq), NEG, F32)), scores(0),
            rep(jnp.zeros((hd + ONES_ROWS, tq), F32)))
    corr, m, bmax, acc = lax.fori_loop(0, nfull, body, init)
    acc = pv(jnp.maximum(nfull - 1, 0), corr, acc)
    for d in range(tq // tk):
        if d > 0:
            scores(nfull + d)
        corr, m = softmax(nfull + d, m, None, masked=True)
        acc = pv(nfull + d, corr, acc)
    ot = jnp.concatenate([a[0:hd] / a[hd:hd + 1] for a in acc], axis=0)
    o_ref[...] = ot.T.astype(o_ref.dtype)


def _fox_attention(qt, k, vt, caug):
    bsz, seq, _ = k.shape
    tq, tk = FOX_TQ, FOX_TK
    wide = FOX_PAIRS * LANES
    nh = 2 * FOX_PAIRS
    return pl.pallas_call(
        functools.partial(_fox_kernel, tq=tq, tk=tk),
        grid=(bsz, FOX_WIDTH // wide, seq // tq),
        in_specs=[
            pl.BlockSpec((None, wide, tq), lambda b, p, i: (b, p, i)),
            pl.BlockSpec((None, seq, wide), lambda b, p, i: (b, 0, p)),
            pl.BlockSpec((None, seq, LANES), lambda b, p, i: (b, 0, 0)),
            pl.BlockSpec((None, seq // tk, wide, tk), lambda b, p, i: (b, 0, p, 0)),
        ],
        out_specs=pl.BlockSpec((None, tq, wide), lambda b, p, i: (b, i, p)),
        out_shape=jax.ShapeDtypeStruct((bsz, seq, FOX_WIDTH), BF16),
        scratch_shapes=[pltpu.VMEM((nh, tk, tq), F32), pltpu.VMEM((nh, tk, tq), BF16)],
        compiler_params=_params("parallel", "parallel", "arbitrary"),
        name="fox_attention",
    )(qt, k, caug, vt)


def _out_proj_kernel(*refs, final):
    if final:
        o_ref, gate_ref, h_ref, w_ref, gf_ref, out_ref = refs
    else:
        o_ref, gate_ref, h_ref, w_ref, out_ref = refs
    g = gate_ref[...].astype(F32)
    a = o_ref[...].astype(F32) * (g * (1.0 / (1.0 + jnp.exp(-g))))
    y = h_ref[...] + jnp.dot(a.astype(BF16), w_ref[...], preferred_element_type=F32)
    if final:
        y = _rms(y, gf_ref[...])
    out_ref[...] = y


def _out_proj(o, gate, h, w, final_gain=None):
    bsz, seq, dm = h.shape
    width = o.shape[-1]
    tm = ROW_TILE
    row_spec = lambda wd: pl.BlockSpec((None, tm, wd), lambda b, i: (b, i, 0))
    in_specs = [row_spec(width), row_spec(width), row_spec(dm),
                pl.BlockSpec((width, dm), lambda b, i: (0, 0))]
    args = [o, gate, h, w.astype(BF16)]
    if final_gain is not None:
        in_specs.append(pl.BlockSpec((1, dm), lambda b, i: (0, 0)))
        args.append(final_gain.reshape(1, dm).astype(F32))
    return pl.pallas_call(
        functools.partial(_out_proj_kernel, final=final_gain is not None),
        grid=(bsz, seq // tm),
        in_specs=in_specs,
        out_specs=row_spec(dm),
        out_shape=jax.ShapeDtypeStruct((bsz, seq, dm), F32),
        compiler_params=_params("parallel", "parallel"),
        name="out_proj_final" if final_gain is not None else "out_proj",
    )(*args)


def _t5_bucket(dist):
    d = np.asarray(dist)
    exact = REL_BUCKETS // 2
    large = exact + (np.log(np.maximum(d, 1) / exact) / np.log(REL_MAX_DIST / exact)
                     * (REL_BUCKETS - exact)).astype(np.int32)
    large = np.minimum(large, REL_BUCKETS - 1)
    return np.where(d < exact, d, large).astype(np.int32)


def _bucket_ids():
    qi = np.arange(BLK)[:, None]
    ki = np.arange(2 * BLK)[None, :]
    rel = qi + BLK - ki
    out = []
    for window, dilation in DIL_GROUPS:
        steps = window // dilation
        in_band = (rel >= 0) & (rel <= steps)
        b = _t5_bucket(np.clip(rel, 0, steps) * dilation)
        out.append(np.where(in_band, b, REL_BUCKETS))
    return np.stack(out).astype(np.int32)


def _bias_kernel(table_ref, bucket_ref, out_ref):
    gh = pl.program_id(0)
    bk = bucket_ref[gh // DIL_HEADS]
    acc = jnp.full(bk.shape, NEG, F32)
    for b in range(REL_BUCKETS):
        acc = jnp.where(bk == b, table_ref[b, gh], acc)
    col = lax.broadcasted_iota(jnp.int32, bk.shape, 1)
    out_ref[0] = acc
    out_ref[1] = jnp.where(col < BLK, NEG, acc)


def _bias_tables(rel_bias):
    ngh = N_GROUPS * DIL_HEADS
    return pl.pallas_call(
        _bias_kernel,
        grid=(ngh,),
        in_specs=[
            pl.BlockSpec(memory_space=pltpu.SMEM),
            pl.BlockSpec((N_GROUPS, BLK, 2 * BLK), lambda i: (0, 0, 0)),
        ],
        out_specs=pl.BlockSpec((None, 2, BLK, 2 * BLK), lambda i: (i, 0, 0, 0)),
        out_shape=jax.ShapeDtypeStruct((ngh, 2, BLK, 2 * BLK), F32),
        compiler_params=_params("arbitrary"),
        name="t5_bias_tables",
    )(rel_bias.astype(F32), jnp.asarray(_bucket_ids()))


def _dilated_kernel(*refs):
    q_refs = refs[0:3]
    kv_refs = [refs[3 + 4 * g:7 + 4 * g] for g in range(N_GROUPS)]
    bias_ref = refs[15]
    o_ref = refs[16]
    acc_s, m_s, l_s = refs[17:20]
    first = jnp.where(pl.program_id(2) == 0, 1, 0)

    def rows(ref, start, size, stride):
        if stride == 1:
            return ref[pl.ds(start, size), :]
        return ref[pl.ds(start, size, stride=stride), :]

    for g, (window, dil) in enumerate(DIL_GROUPS):
        span = BLK * dil
        kprev_ref, kcur_ref, vprev_ref, vcur_ref = kv_refs[g]
        for j in range(DIL_BLOCK // span):
            def unit(r, _, g=g, dil=dil, span=span, j=j, kprev_ref=kprev_ref, kcur_ref=kcur_ref,
                     vprev_ref=vprev_ref, vcur_ref=vcur_ref):
                start = j * span + r
                q = rows(q_refs[g], start, BLK, dil).astype(BF16)
                if j == 0:
                    k = jnp.concatenate([rows(kprev_ref, r, BLK, dil), rows(kcur_ref, r, BLK, dil)], axis=0)
                    v = jnp.concatenate([rows(vprev_ref, r, BLK, dil), rows(vcur_ref, r, BLK, dil)], axis=0)
                    bias = bias_ref[g, first]
                else:
                    k = rows(kcur_ref, (j - 1) * span + r, 2 * BLK, dil)
                    v = rows(vcur_ref, (j - 1) * span + r, 2 * BLK, dil)
                    bias = bias_ref[g, 0]
                s = lax.dot_general(q, k.astype(BF16), _NT, preferred_element_type=F32) + bias
                m = jnp.max(s, axis=-1, keepdims=True)
                p = jnp.exp(s - m)
                l = jnp.sum(p, axis=-1, keepdims=True)
                acc = jnp.dot(p.astype(BF16), v.astype(BF16), preferred_element_type=F32)
                if dil == 1:
                    dst = pl.ds(start, BLK)
                else:
                    dst = pl.ds(start, BLK, stride=dil)
                acc_s[g, dst, :] = acc
                m_s[g, dst, :] = jnp.broadcast_to(m, (BLK, LANES))
                l_s[g, dst, :] = jnp.broadcast_to(l, (BLK, LANES))
                return 0

            for r in range(dil):
                unit(r, 0)

    chunk = 256

    def merge(i, _):
        sl = pl.ds(pl.multiple_of(i * chunk, chunk), chunk)
        ms = [m_s[g, sl, :] for g in range(N_GROUPS)]
        m_all = jnp.maximum(jnp.maximum(ms[0], ms[1]), ms[2])
        num = jnp.zeros((chunk, LANES), F32)
        den = jnp.zeros((chunk, LANES), F32)
        for g in range(N_GROUPS):
            w = jnp.exp(ms[g] - m_all)
            num = num + w * acc_s[g, sl, :]
            den = den + w * l_s[g, sl, :]
        o_ref[sl, :] = (num / den).astype(o_ref.dtype)
        return 0

    lax.fori_loop(0, DIL_BLOCK // chunk, merge, 0)


def _dilated_attention(uq, kv, bias):
    bsz, seq, _ = uq.shape
    nblk = seq // DIL_BLOCK
    in_specs = []
    args = []
    for g in range(N_GROUPS):
        in_specs.append(pl.BlockSpec((None, DIL_BLOCK, LANES),
                                     lambda b, h, n, g=g: (b, n, g * DIL_HEADS + h)))
        args.append(uq)
    for g, (_, dil) in enumerate(DIL_GROUPS):
        span = BLK * dil
        per = DIL_BLOCK // span
        for t in range(2):
            lane_blk = (2 * g + t) * DIL_HEADS
            in_specs.append(pl.BlockSpec(
                (None, span, LANES),
                lambda b, h, n, per=per, lane_blk=lane_blk: (b, jnp.maximum(n * per - 1, 0), lane_blk + h)))
            in_specs.append(pl.BlockSpec(
                (None, DIL_BLOCK, LANES), lambda b, h, n, lane_blk=lane_blk: (b, n, lane_blk + h)))
            args += [kv, kv]
    bias5 = bias.reshape(N_GROUPS, DIL_HEADS, 2, BLK, 2 * BLK)
    in_specs.append(pl.BlockSpec((N_GROUPS, None, 2, BLK, 2 * BLK), lambda b, h, n: (0, h, 0, 0, 0)))
    args.append(bias5)
    state = pltpu.VMEM((N_GROUPS, DIL_BLOCK, LANES), F32)
    return pl.pallas_call(
        _dilated_kernel,
        grid=(bsz, DIL_HEADS, nblk),
        in_specs=in_specs,
        out_specs=pl.BlockSpec((None, DIL_BLOCK, LANES), lambda b, h, n: (b, n, h)),
        out_shape=jax.ShapeDtypeStruct((bsz, seq, DIL_WIDTH), BF16),
        scratch_shapes=[state, state, state],
        compiler_params=_params("parallel", "parallel", "arbitrary"),
        name="dilated_attention",
    )(*args)


def kernel(x, rel_bias, norm_a, w_in_a, b_f_a, w_out_a, norm_kv, w_kv, norm_b, w_in_b, w_out_b, norm_f):
    bsz, seq, dm = x.shape
    assert dm == D_MODEL and seq % DIL_BLOCK == 0 and seq % ROW_TILE == 0
    n_a = w_in_a.shape[0]
    n_b = w_in_b.shape[0]
    h = x
    for i in range(n_a):
        k, gate, qt, vt, caug = _fox_proj(h, norm_a[i], w_in_a[i], b_f_a[i])
        o = _fox_attention(qt, k, vt, caug)
        h = _out_proj(o, gate, h, w_out_a[i])

    (kv,) = _norm_proj(h, norm_kv, w_kv, ((0, w_kv.shape[1], 1.0, F32),))
    bias = _bias_tables(rel_bias)
    gw = N_GROUPS * DIL_WIDTH
    for i in range(n_b):
        segs = ((0, gw, DIL_HEAD_DIM ** -0.5, F32), (gw, DIL_WIDTH, 1.0, BF16))
        uq, gate = _norm_proj(h, norm_b[i], w_in_b[i], segs)
        o = _dilated_attention(uq, kv, bias)
        h = _out_proj(o, gate, h, w_out_b[i], final_gain=norm_f if i == n_b - 1 else None)
    return h
```

```python
import functools

import jax
import jax.numpy as jnp
import numpy as np
from jax import lax
from jax.experimental import pallas as pl
from jax.experimental.pallas import tpu as pltpu

D_MODEL = 1024
BLK = 128
FOX_HEADS = 16
FOX_HEAD_DIM = D_MODEL // FOX_HEADS
FOX_WIDTH = FOX_HEADS * FOX_HEAD_DIM
DIL_GROUPS = ((128, 1), (512, 4), (2048, 16))
N_GROUPS = len(DIL_GROUPS)
DIL_HEADS = 8
DIL_HEAD_DIM = D_MODEL // DIL_HEADS
DIL_WIDTH = DIL_HEADS * DIL_HEAD_DIM
REL_BUCKETS = 32
REL_MAX_DIST = 2048
RMS_EPS = 1e-6
NEG = -1e30
LOG2E = 1.4426950408889634

LANES = 128
VMEM_LIMIT = 56 * 1024 * 1024
ROW_TILE = 512
COL_CHUNK = 1024
FOX_TK = ROW_TILE
FOX_TQ = 512
FOX_PAIRS = 2
ONES_ROWS = 16
CAUG_GROUP = FOX_HEADS
DIL_BLOCK = BLK * max(d for _, d in DIL_GROUPS)

F32 = jnp.float32
BF16 = jnp.bfloat16
_NT = (((1,), (1,)), ((), ()))


def _params(*sem):
    return pltpu.CompilerParams(dimension_semantics=sem, vmem_limit_bytes=VMEM_LIMIT)


def _rms(x, g):
    return x * lax.rsqrt(jnp.mean(x * x, axis=-1, keepdims=True) + RMS_EPS) * g


def _split3(x):
    p1 = x.astype(BF16)
    r1 = x - p1.astype(F32)
    p2 = r1.astype(BF16)
    p3 = (r1 - p2.astype(F32)).astype(BF16)
    return p1, p2, p3


def _norm_proj_kernel(*refs, segs):
    x_ref, g_ref, w_ref = refs[:3]
    out_refs = refs[3:3 + len(segs)]
    xh = _rms(x_ref[...], g_ref[...]).astype(BF16)
    for (start, width, scale), o_ref in zip(segs, out_refs):
        for c in range(0, width, COL_CHUNK):
            cw = min(COL_CHUNK, width - c)
            y = jnp.dot(xh, w_ref[:, start + c:start + c + cw], preferred_element_type=F32)
            if scale != 1.0:
                y = y * scale
            o_ref[:, c:c + cw] = y.astype(o_ref.dtype)


def _norm_proj(h, g, w, segs):
    bsz, seq, dm = h.shape
    tm = ROW_TILE
    n_in = w.shape[1]
    kern_segs = tuple((s, wd, sc) for s, wd, sc, _ in segs)
    return pl.pallas_call(
        functools.partial(_norm_proj_kernel, segs=kern_segs),
        grid=(bsz, seq // tm),
        in_specs=[
            pl.BlockSpec((None, tm, dm), lambda b, i: (b, i, 0)),
            pl.BlockSpec((1, dm), lambda b, i: (0, 0)),
            pl.BlockSpec((dm, n_in), lambda b, i: (0, 0)),
        ],
        out_specs=[pl.BlockSpec((None, tm, wd), lambda b, i: (b, i, 0)) for _, wd, _, _ in segs],
        out_shape=[jax.ShapeDtypeStruct((bsz, seq, wd), dt) for _, wd, _, dt in segs],
        compiler_params=_params("parallel", "parallel"),
        name="norm_proj",
    )(h, g.reshape(1, dm).astype(F32), w.astype(BF16))


def _fox_proj_kernel(x_ref, g_ref, wn_ref, wt_ref, wfh_ref, wfl_ref, bf_ref,
                     k_ref, gate_ref, qt_ref, vt_ref, caug_ref, carry_ref):
    tm = x_ref.shape[0]
    w = FOX_WIDTH
    xn = _rms(x_ref[...], g_ref[...])
    xh = xn.astype(BF16)
    k_ref[...] = jnp.dot(xh, wn_ref[:, 0:w], preferred_element_type=F32).astype(BF16)
    gate_ref[...] = jnp.dot(xh, wn_ref[:, w:2 * w], preferred_element_type=F32).astype(BF16)
    rows = 256
    qscale = FOX_HEAD_DIM ** -0.5 * LOG2E
    for r in range(0, w, rows):
        y = lax.dot_general(wt_ref[r:r + rows, :], xh, _NT, preferred_element_type=F32)
        qt_ref[r:r + rows, :] = (y * qscale).astype(BF16)
    for r in range(0, w, rows):
        y = lax.dot_general(wt_ref[w + r:w + r + rows, :], xh, _NT, preferred_element_type=F32)
        vt_ref[r:r + rows, :] = y.astype(BF16)

    @pl.when(pl.program_id(1) == 0)
    def _():
        carry_ref[...] = jnp.zeros_like(carry_ref)

    xl = (xn - xh.astype(F32)).astype(BF16)
    wfh = wfh_ref[...]
    f = (jnp.dot(xh, wfh, preferred_element_type=F32) + jnp.dot(xl, wfh, preferred_element_type=F32)
         + jnp.dot(xh, wfl_ref[...], preferred_element_type=F32)) + bf_ref[...]
    ls = jnp.minimum(f, 0.0) - jnp.log1p(jnp.exp(-jnp.abs(f)))
    row = lax.broadcasted_iota(jnp.int32, (tm, tm), 0)
    col = lax.broadcasted_iota(jnp.int32, (tm, tm), 1)
    tri = jnp.where(row >= col, 1.0, 0.0).astype(BF16)
    p1, p2, p3 = _split3(ls)
    c = (jnp.dot(tri, p1, preferred_element_type=F32) + jnp.dot(tri, p2, preferred_element_type=F32)
         + jnp.dot(tri, p3, preferred_element_type=F32)) + carry_ref[0:1, :]
    carry_ref[...] = jnp.broadcast_to(c[tm - 1:tm, :], carry_ref.shape)
    d1, d2, d3 = _split3(c * (-LOG2E))
    lane = lax.broadcasted_iota(jnp.int32, (tm, LANES), 1)
    caug_ref[...] = jnp.where(lane < CAUG_GROUP, d1,
                              jnp.where(lane < 2 * CAUG_GROUP, d2,
                                        jnp.where(lane < 3 * CAUG_GROUP, d3, jnp.zeros_like(d1))))


def _fox_proj(h, g, w_in, b_f):
    bsz, seq, dm = h.shape
    tm = ROW_TILE
    w = FOX_WIDTH
    wn = jnp.concatenate([w_in[:, w:2 * w], w_in[:, 3 * w:4 * w]], axis=1).astype(BF16)
    wt = jnp.concatenate([w_in[:, 0:w], w_in[:, 2 * w:3 * w]], axis=1).T.astype(BF16)
    w_f = w_in[:, 4 * w:].astype(F32)
    pad = jnp.zeros((dm, LANES - 3 * FOX_HEADS), F32)
    wf_rep = jnp.concatenate([w_f, w_f, w_f, pad], axis=1)
    wfh = wf_rep.astype(BF16)
    wfl = (wf_rep - wfh.astype(F32)).astype(BF16)
    bf = b_f.astype(F32)
    bf_rep = jnp.concatenate([bf, bf, bf, jnp.zeros((LANES - 3 * FOX_HEADS,), F32)]).reshape(1, LANES)
    const = lambda shape: pl.BlockSpec(shape, lambda b, i: (0,) * len(shape))
    return pl.pallas_call(
        _fox_proj_kernel,
        grid=(bsz, seq // tm),
        in_specs=[
            pl.BlockSpec((None, tm, dm), lambda b, i: (b, i, 0)),
            const((1, dm)), const((dm, 2 * w)), const((2 * w, dm)),
            const((dm, LANES)), const((dm, LANES)), const((1, LANES)),
        ],
        out_specs=[
            pl.BlockSpec((None, tm, w), lambda b, i: (b, i, 0)),
            pl.BlockSpec((None, tm, w), lambda b, i: (b, i, 0)),
            pl.BlockSpec((None, w, tm), lambda b, i: (b, 0, i)),
            pl.BlockSpec((None, None, w, tm), lambda b, i: (b, i, 0, 0)),
            pl.BlockSpec((None, tm, LANES), lambda b, i: (b, i, 0)),
        ],
        out_shape=[
            jax.ShapeDtypeStruct((bsz, seq, w), BF16),
            jax.ShapeDtypeStruct((bsz, seq, w), BF16),
            jax.ShapeDtypeStruct((bsz, w, seq), BF16),
            jax.ShapeDtypeStruct((bsz, seq // tm, w, tm), BF16),
            jax.ShapeDtypeStruct((bsz, seq, LANES), BF16),
        ],
        scratch_shapes=[pltpu.VMEM((8, LANES), F32)],
        compiler_params=_params("parallel", "arbitrary"),
        name="fox_proj",
    )(h, g.reshape(1, dm).astype(F32), wn, wt, wfh, wfl, bf_rep)


def _fox_kernel(qt_ref, k_ref, c_ref, vt_ref, o_ref, st_s, p_s, *, tq, tk):
    grp = pl.program_id(1)
    qi = pl.program_id(2)
    hd = FOX_HEAD_DIM
    nh = 2 * FOX_PAIRS
    row = lax.broadcasted_iota(jnp.int32, (LANES, tq), 0)
    wq = []
    for i in range(nh):
        head = nh * grp + i
        qt = qt_ref[(i // 2) * LANES:(i // 2 + 1) * LANES, :]
        qh = jnp.where((row < hd) if i % 2 == 0 else (row >= hd), qt, jnp.zeros_like(qt))
        aug = jnp.where(((row & (CAUG_GROUP - 1)) == head) & (row < 3 * CAUG_GROUP), 1.0, 0.0)
        wq.append(jnp.concatenate([qh, aug.astype(BF16)], axis=0))
    ones = jnp.ones((ONES_ROWS, tk), BF16)
    krow = lax.broadcasted_iota(jnp.int32, (tk, tq), 0)
    qcol = lax.broadcasted_iota(jnp.int32, (tk, tq), 1)

    def scores(j):
        start = pl.multiple_of(j * tk, tk)
        c = c_ref[pl.ds(start, tk), :]
        lhs = [jnp.concatenate([k_ref[pl.ds(start, tk), pp * LANES:(pp + 1) * LANES], c], axis=1)
               for pp in range(FOX_PAIRS)]
        bmax = []
        for i in range(nh):
            s = jnp.dot(lhs[i // 2], wq[i], preferred_element_type=F32)
            st_s[i] = s
            bmax.append(jnp.max(s, axis=0, keepdims=True))
        return tuple(bmax)

    def softmax(j, m, bmax, masked):
        corrs, ms = [], []
        for i in range(nh):
            s = st_s[i]
            if masked:
                s = jnp.where(krow <= qcol, s, NEG)
                m_new = jnp.maximum(m[i], jnp.max(s, axis=0, keepdims=True))
            else:
                m_new = jnp.maximum(m[i], bmax[i])
            p_s[i] = jnp.exp2(s - m_new).astype(BF16)
            corrs.append(jnp.exp2(m[i] - m_new))
            ms.append(m_new)
        return tuple(corrs), tuple(ms)

    def pv(j, corr, acc):
        out = []
        for i in range(nh):
            vv = jnp.concatenate([vt_ref[j, i * hd:(i + 1) * hd, :], ones], axis=0)
            out.append(acc[i] * corr[i] + jnp.dot(vv, p_s[i], preferred_element_type=F32))
        return tuple(out)

    def body(j, carry):
        corr, m, bmax, acc = carry
        acc = pv(jnp.where(j == 0, nfull, j - 1), corr, acc)
        corr, m = softmax(j, m, bmax, masked=False)
        bmax = scores(j + 1)
        return corr, m, bmax, acc

    nfull = qi
    rep = lambda v: (v,) * nh
    scores(nfull)
    corr, m = softmax(nfull, rep(jnp.full((1, tq), NEG, F32)), None, masked=True)
    init = (corr, m, scores(0), rep(jnp.zeros((hd + ONES_ROWS, tq), F32)))
    corr, m, bmax, acc = lax.fori_loop(0, nfull, body, init)
    acc = pv(jnp.maximum(nfull - 1, 0), corr, acc)
    ot = jnp.concatenate([a[0:hd] / a[hd:hd + 1] for a in acc], axis=0)
    o_ref[...] = ot.T.astype(o_ref.dtype)


def _fox_attention(qt, k, vt, caug):
    bsz, seq, _ = k.shape
    tq, tk = FOX_TQ, FOX_TK
    assert tq == tk, "one masked diagonal block per query block"
    wide = FOX_PAIRS * LANES
    nh = 2 * FOX_PAIRS
    return pl.pallas_call(
        functools.partial(_fox_kernel, tq=tq, tk=tk),
        grid=(bsz, FOX_WIDTH // wide, seq // tq),
        in_specs=[
            pl.BlockSpec((None, wide, tq), lambda b, p, i: (b, p, i)),
            pl.BlockSpec((None, seq, wide), lambda b, p, i: (b, 0, p)),
            pl.BlockSpec((None, seq, LANES), lambda b, p, i: (b, 0, 0)),
            pl.BlockSpec((None, seq // tk, wide, tk), lambda b, p, i: (b, 0, p, 0)),
        ],
        out_specs=pl.BlockSpec((None, tq, wide), lambda b, p, i: (b, i, p)),
        out_shape=jax.ShapeDtypeStruct((bsz, seq, FOX_WIDTH), BF16),
        scratch_shapes=[pltpu.VMEM((nh, tk, tq), F32), pltpu.VMEM((nh, tk, tq), BF16)],
        compiler_params=_params("parallel", "parallel", "arbitrary"),
        name="fox_attention",
    )(qt, k, caug, vt)


def _out_proj_kernel(*refs, final):
    if final:
        o_ref, gate_ref, h_ref, w_ref, gf_ref, out_ref = refs
    else:
        o_ref, gate_ref, h_ref, w_ref, out_ref = refs
    g = gate_ref[...].astype(F32)
    a = o_ref[...].astype(F32) * (g * (1.0 / (1.0 + jnp.exp(-g))))
    y = h_ref[...] + jnp.dot(a.astype(BF16), w_ref[...], preferred_element_type=F32)
    if final:
        y = _rms(y, gf_ref[...])
    out_ref[...] = y


def _out_proj(o, gate, h, w, final_gain=None):
    bsz, seq, dm = h.shape
    width = o.shape[-1]
    tm = ROW_TILE
    row_spec = lambda wd: pl.BlockSpec((None, tm, wd), lambda b, i: (b, i, 0))
    in_specs = [row_spec(width), row_spec(width), row_spec(dm),
                pl.BlockSpec((width, dm), lambda b, i: (0, 0))]
    args = [o, gate, h, w.astype(BF16)]
    if final_gain is not None:
        in_specs.append(pl.BlockSpec((1, dm), lambda b, i: (0, 0)))
        args.append(final_gain.reshape(1, dm).astype(F32))
    return pl.pallas_call(
        functools.partial(_out_proj_kernel, final=final_gain is not None),
        grid=(bsz, seq // tm),
        in_specs=in_specs,
        out_specs=row_spec(dm),
        out_shape=jax.ShapeDtypeStruct((bsz, seq, dm), F32),
        compiler_params=_params("parallel", "parallel"),
        name="out_proj_final" if final_gain is not None else "out_proj",
    )(*args)


def _t5_bucket(dist):
    d = np.asarray(dist)
    exact = REL_BUCKETS // 2
    large = exact + (np.log(np.maximum(d, 1) / exact) / np.log(REL_MAX_DIST / exact)
                     * (REL_BUCKETS - exact)).astype(np.int32)
    large = np.minimum(large, REL_BUCKETS - 1)
    return np.where(d < exact, d, large).astype(np.int32)


def _bucket_ids():
    qi = np.arange(BLK)[:, None]
    ki = np.arange(2 * BLK)[None, :]
    rel = qi + BLK - ki
    out = []
    for window, dilation in DIL_GROUPS:
        steps = window // dilation
        in_band = (rel >= 0) & (rel <= steps)
        b = _t5_bucket(np.clip(rel, 0, steps) * dilation)
        out.append(np.where(in_band, b, REL_BUCKETS))
    return np.stack(out).astype(np.int32)


def _bias_kernel(table_ref, bucket_ref, out_ref):
    gh = pl.program_id(0)
    bk = bucket_ref[gh // DIL_HEADS]
    acc = jnp.full(bk.shape, NEG, F32)
    for b in range(REL_BUCKETS):
        acc = jnp.where(bk == b, table_ref[b, gh], acc)
    col = lax.broadcasted_iota(jnp.int32, bk.shape, 1)
    out_ref[0] = acc
    out_ref[1] = jnp.where(col < BLK, NEG, acc)


def _bias_tables(rel_bias):
    ngh = N_GROUPS * DIL_HEADS
    return pl.pallas_call(
        _bias_kernel,
        grid=(ngh,),
        in_specs=[
            pl.BlockSpec(memory_space=pltpu.SMEM),
            pl.BlockSpec((N_GROUPS, BLK, 2 * BLK), lambda i: (0, 0, 0)),
        ],
        out_specs=pl.BlockSpec((None, 2, BLK, 2 * BLK), lambda i: (i, 0, 0, 0)),
        out_shape=jax.ShapeDtypeStruct((ngh, 2, BLK, 2 * BLK), F32),
        compiler_params=_params("arbitrary"),
        name="t5_bias_tables",
    )(rel_bias.astype(F32), jnp.asarray(_bucket_ids()))


def _dilated_kernel(*refs):
    q_refs = refs[0:3]
    kv_refs = [refs[3 + 4 * g:7 + 4 * g] for g in range(N_GROUPS)]
    bias_ref = refs[15]
    o_ref = refs[16]
    acc_s, m_s, l_s = refs[17:20]
    first = jnp.where(pl.program_id(2) == 0, 1, 0)

    def rows(ref, start, size, stride):
        if stride == 1:
            return ref[pl.ds(start, size), :]
        return ref[pl.ds(start, size, stride=stride), :]

    for g, (window, dil) in enumerate(DIL_GROUPS):
        span = BLK * dil
        kprev_ref, kcur_ref, vprev_ref, vcur_ref = kv_refs[g]
        for j in range(DIL_BLOCK // span):
            def unit(r, _, g=g, dil=dil, span=span, j=j, kprev_ref=kprev_ref, kcur_ref=kcur_ref,
                     vprev_ref=vprev_ref, vcur_ref=vcur_ref):
                start = j * span + r
                q = rows(q_refs[g], start, BLK, dil).astype(BF16)
                if j == 0:
                    k = jnp.concatenate([rows(kprev_ref, r, BLK, dil), rows(kcur_ref, r, BLK, dil)], axis=0)
                    v = jnp.concatenate([rows(vprev_ref, r, BLK, dil), rows(vcur_ref, r, BLK, dil)], axis=0)
                    bias = bias_ref[g, first]
                else:
                    k = rows(kcur_ref, (j - 1) * span + r, 2 * BLK, dil)
                    v = rows(vcur_ref, (j - 1) * span + r, 2 * BLK, dil)
                    bias = bias_ref[g, 0]
                s = lax.dot_general(q, k.astype(BF16), _NT, preferred_element_type=F32) + bias
                m = jnp.max(s, axis=-1, keepdims=True)
                p = jnp.exp(s - m)
                l = jnp.sum(p, axis=-1, keepdims=True)
                acc = jnp.dot(p.astype(BF16), v.astype(BF16), preferred_element_type=F32)
                if dil == 1:
                    dst = pl.ds(start, BLK)
                else:
                    dst = pl.ds(start, BLK, stride=dil)
                acc_s[g, dst, :] = acc
                m_s[g, dst, :] = jnp.broadcast_to(m, (BLK, LANES))
                l_s[g, dst, :] = jnp.broadcast_to(l, (BLK, LANES))
                return 0

            for r in range(dil):
                unit(r, 0)

    chunk = 256

    def merge(i, _):
        sl = pl.ds(pl.multiple_of(i * chunk, chunk), chunk)
        ms = [m_s[g, sl, :] for g in range(N_GROUPS)]
        m_all = jnp.maximum(jnp.maximum(ms[0], ms[1]), ms[2])
        num = jnp.zeros((chunk, LANES), F32)
        den = jnp.zeros((chunk, LANES), F32)
        for g in range(N_GROUPS):
            w = jnp.exp(ms[g] - m_all)
            num = num + w * acc_s[g, sl, :]
            den = den + w * l_s[g, sl, :]
        o_ref[sl, :] = (num / den).astype(o_ref.dtype)
        return 0

    lax.fori_loop(0, DIL_BLOCK // chunk, merge, 0)


def _dilated_attention(uq, kv, bias):
    bsz, seq, _ = uq.shape
    nblk = seq // DIL_BLOCK
    in_specs = []
    args = []
    for g in range(N_GROUPS):
        in_specs.append(pl.BlockSpec((None, DIL_BLOCK, LANES),
                                     lambda b, h, n, g=g: (b, n, g * DIL_HEADS + h)))
        args.append(uq)
    for g, (_, dil) in enumerate(DIL_GROUPS):
        span = BLK * dil
        per = DIL_BLOCK // span
        for t in range(2):
            lane_blk = (2 * g + t) * DIL_HEADS
            in_specs.append(pl.BlockSpec(
                (None, span, LANES),
                lambda b, h, n, per=per, lane_blk=lane_blk: (b, jnp.maximum(n * per - 1, 0), lane_blk + h)))
            in_specs.append(pl.BlockSpec(
                (None, DIL_BLOCK, LANES), lambda b, h, n, lane_blk=lane_blk: (b, n, lane_blk + h)))
            args += [kv, kv]
    bias5 = bias.reshape(N_GROUPS, DIL_HEADS, 2, BLK, 2 * BLK)
    in_specs.append(pl.BlockSpec((N_GROUPS, None, 2, BLK, 2 * BLK), lambda b, h, n: (0, h, 0, 0, 0)))
    args.append(bias5)
    state = pltpu.VMEM((N_GROUPS, DIL_BLOCK, LANES), F32)
    return pl.pallas_call(
        _dilated_kernel,
        grid=(bsz, DIL_HEADS, nblk),
        in_specs=in_specs,
        out_specs=pl.BlockSpec((None, DIL_BLOCK, LANES), lambda b, h, n: (b, n, h)),
        out_shape=jax.ShapeDtypeStruct((bsz, seq, DIL_WIDTH), BF16),
        scratch_shapes=[state, state, state],
        compiler_params=_params("parallel", "parallel", "arbitrary"),
        name="dilated_attention",
    )(*args)


def kernel(x, rel_bias, norm_a, w_in_a, b_f_a, w_out_a, norm_kv, w_kv, norm_b, w_in_b, w_out_b, norm_f):
    bsz, seq, dm = x.shape
    assert dm == D_MODEL and seq % DIL_BLOCK == 0 and seq % ROW_TILE == 0
    n_a = w_in_a.shape[0]
    n_b = w_in_b.shape[0]
    h = x
    for i in range(n_a):
        k, gate, qt, vt, caug = _fox_proj(h, norm_a[i], w_in_a[i], b_f_a[i])
        o = _fox_attention(qt, k, vt, caug)
        h = _out_proj(o, gate, h, w_out_a[i])

    (kv,) = _norm_proj(h, norm_kv, w_kv, ((0, w_kv.shape[1], 1.0, F32),))
    bias = _bias_tables(rel_bias)
    gw = N_GROUPS * DIL_WIDTH
    for i in range(n_b):
        segs = ((0, gw, DIL_HEAD_DIM ** -0.5, F32), (gw, DIL_WIDTH, 1.0, BF16))
        uq, gate = _norm_proj(h, norm_b[i], w_in_b[i], segs)
        o = _dilated_attention(uq, kv, bias)
        h = _out_proj(o, gate, h, w_out_b[i], final_gain=norm_f if i == n_b - 1 else None)
    return h
```

```python
import functools

import jax
import jax.numpy as jnp
import numpy as np
from jax import lax
from jax.experimental import pallas as pl
from jax.experimental.pallas import tpu as pltpu

D_MODEL = 1024
BLK = 128
FOX_HEADS = 16
FOX_HEAD_DIM = D_MODEL // FOX_HEADS
FOX_WIDTH = FOX_HEADS * FOX_HEAD_DIM
DIL_GROUPS = ((128, 1), (512, 4), (2048, 16))
N_GROUPS = len(DIL_GROUPS)
DIL_HEADS = 8
DIL_HEAD_DIM = D_MODEL // DIL_HEADS
DIL_WIDTH = DIL_HEADS * DIL_HEAD_DIM
REL_BUCKETS = 32
REL_MAX_DIST = 2048
RMS_EPS = 1e-6
NEG = -1e30
LOG2E = 1.4426950408889634

LANES = 128
VMEM_LIMIT = 56 * 1024 * 1024
ROW_TILE = 512
COL_CHUNK = 1024
FOX_TK = ROW_TILE
FOX_TQ = 512
FOX_PAIRS = 2
ONES_ROWS = 16
CAUG_GROUP = FOX_HEADS
DIL_BLOCK = BLK * max(d for _, d in DIL_GROUPS)

F32 = jnp.float32
BF16 = jnp.bfloat16
_NT = (((1,), (1,)), ((), ()))


def _params(*sem):
    return pltpu.CompilerParams(dimension_semantics=sem, vmem_limit_bytes=VMEM_LIMIT)


def _rms(x, g):
    return x * lax.rsqrt(jnp.mean(x * x, axis=-1, keepdims=True) + RMS_EPS) * g


def _split3(x):
    p1 = x.astype(BF16)
    r1 = x - p1.astype(F32)
    p2 = r1.astype(BF16)
    p3 = (r1 - p2.astype(F32)).astype(BF16)
    return p1, p2, p3


def _norm_proj_kernel(*refs, segs):
    x_ref, g_ref, w_ref = refs[:3]
    out_refs = refs[3:3 + len(segs)]
    xh = _rms(x_ref[...], g_ref[...]).astype(BF16)
    for (start, width, scale), o_ref in zip(segs, out_refs):
        for c in range(0, width, COL_CHUNK):
            cw = min(COL_CHUNK, width - c)
            y = jnp.dot(xh, w_ref[:, start + c:start + c + cw], preferred_element_type=F32)
            if scale != 1.0:
                y = y * scale
            o_ref[:, c:c + cw] = y.astype(o_ref.dtype)


def _norm_proj(h, g, w, segs):
    bsz, seq, dm = h.shape
    tm = ROW_TILE
    n_in = w.shape[1]
    kern_segs = tuple((s, wd, sc) for s, wd, sc, _ in segs)
    return pl.pallas_call(
        functools.partial(_norm_proj_kernel, segs=kern_segs),
        grid=(bsz, seq // tm),
        in_specs=[
            pl.BlockSpec((None, tm, dm), lambda b, i: (b, i, 0)),
            pl.BlockSpec((1, dm), lambda b, i: (0, 0)),
            pl.BlockSpec((dm, n_in), lambda b, i: (0, 0)),
        ],
        out_specs=[pl.BlockSpec((None, tm, wd), lambda b, i: (b, i, 0)) for _, wd, _, _ in segs],
        out_shape=[jax.ShapeDtypeStruct((bsz, seq, wd), dt) for _, wd, _, dt in segs],
        compiler_params=_params("parallel", "parallel"),
        name="norm_proj",
    )(h, g.reshape(1, dm).astype(F32), w.astype(BF16))


def _fox_proj_kernel(x_ref, g_ref, wn_ref, wt_ref, wf_ref, bf_ref,
                     k_ref, gate_ref, qt_ref, vt_ref, caug_ref, carry_ref):
    tm = x_ref.shape[0]
    w = FOX_WIDTH
    xn = _rms(x_ref[...], g_ref[...])
    xh = xn.astype(BF16)
    k_ref[...] = jnp.dot(xh, wn_ref[:, 0:w], preferred_element_type=F32).astype(BF16)
    gate_ref[...] = jnp.dot(xh, wn_ref[:, w:2 * w], preferred_element_type=F32).astype(BF16)
    rows = 256
    qscale = FOX_HEAD_DIM ** -0.5 * LOG2E
    for r in range(0, w, rows):
        y = lax.dot_general(wt_ref[r:r + rows, :], xh, _NT, preferred_element_type=F32)
        qt_ref[r:r + rows, :] = (y * qscale).astype(BF16)
    for r in range(0, w, rows):
        y = lax.dot_general(wt_ref[w + r:w + r + rows, :], xh, _NT, preferred_element_type=F32)
        vt_ref[r:r + rows, :] = y.astype(BF16)

    @pl.when(pl.program_id(1) == 0)
    def _():
        carry_ref[...] = jnp.zeros_like(carry_ref)

    xl = (xn - xh.astype(F32)).astype(BF16)
    fh = jnp.dot(xh, wf_ref[...], preferred_element_type=F32)
    f = (fh[:, 0:LANES] + jnp.dot(xl, wf_ref[:, 0:LANES], preferred_element_type=F32)
         + fh[:, LANES:2 * LANES]) + bf_ref[...]
    ls = jnp.minimum(f, 0.0) - jnp.log1p(jnp.exp(-jnp.abs(f)))
    row = lax.broadcasted_iota(jnp.int32, (tm, tm), 0)
    col = lax.broadcasted_iota(jnp.int32, (tm, tm), 1)
    tri = jnp.where(row >= col, 1.0, 0.0).astype(BF16)
    cp = jnp.dot(tri, jnp.concatenate(_split3(ls), axis=1), preferred_element_type=F32)
    c = (cp[:, 0:LANES] + cp[:, LANES:2 * LANES] + cp[:, 2 * LANES:3 * LANES]) + carry_ref[0:1, :]
    carry_ref[...] = jnp.broadcast_to(c[tm - 1:tm, :], carry_ref.shape)
    d1, d2, d3 = _split3(c * (-LOG2E))
    lane = lax.broadcasted_iota(jnp.int32, (tm, LANES), 1)
    caug_ref[...] = jnp.where(lane < CAUG_GROUP, d1,
                              jnp.where(lane < 2 * CAUG_GROUP, d2,
                                        jnp.where(lane < 3 * CAUG_GROUP, d3, jnp.zeros_like(d1))))


def _fox_proj(h, g, w_in, b_f):
    bsz, seq, dm = h.shape
    tm = ROW_TILE
    w = FOX_WIDTH
    wn = jnp.concatenate([w_in[:, w:2 * w], w_in[:, 3 * w:4 * w]], axis=1).astype(BF16)
    wt = jnp.concatenate([w_in[:, 0:w], w_in[:, 2 * w:3 * w]], axis=1).T.astype(BF16)
    w_f = w_in[:, 4 * w:].astype(F32)
    pad = jnp.zeros((dm, LANES - 3 * FOX_HEADS), F32)
    wf_rep = jnp.concatenate([w_f, w_f, w_f, pad], axis=1)
    wfh = wf_rep.astype(BF16)
    wf = jnp.concatenate([wfh, (wf_rep - wfh.astype(F32)).astype(BF16)], axis=1)
    bf = b_f.astype(F32)
    bf_rep = jnp.concatenate([bf, bf, bf, jnp.zeros((LANES - 3 * FOX_HEADS,), F32)]).reshape(1, LANES)
    const = lambda shape: pl.BlockSpec(shape, lambda b, i: (0,) * len(shape))
    return pl.pallas_call(
        _fox_proj_kernel,
        grid=(bsz, seq // tm),
        in_specs=[
            pl.BlockSpec((None, tm, dm), lambda b, i: (b, i, 0)),
            const((1, dm)), const((dm, 2 * w)), const((2 * w, dm)),
            const((dm, 2 * LANES)), const((1, LANES)),
        ],
        out_specs=[
            pl.BlockSpec((None, tm, w), lambda b, i: (b, i, 0)),
            pl.BlockSpec((None, tm, w), lambda b, i: (b, i, 0)),
            pl.BlockSpec((None, w, tm), lambda b, i: (b, 0, i)),
            pl.BlockSpec((None, None, w, tm), lambda b, i: (b, i, 0, 0)),
            pl.BlockSpec((None, tm, LANES), lambda b, i: (b, i, 0)),
        ],
        out_shape=[
            jax.ShapeDtypeStruct((bsz, seq, w), BF16),
            jax.ShapeDtypeStruct((bsz, seq, w), BF16),
            jax.ShapeDtypeStruct((bsz, w, seq), BF16),
            jax.ShapeDtypeStruct((bsz, seq // tm, w, tm), BF16),
            jax.ShapeDtypeStruct((bsz, seq, LANES), BF16),
        ],
        scratch_shapes=[pltpu.VMEM((8, LANES), F32)],
        compiler_params=_params("parallel", "arbitrary"),
        name="fox_proj",
    )(h, g.reshape(1, dm).astype(F32), wn, wt, wf, bf_rep)


def _fox_kernel(qt_ref, k_ref, c_ref, vt_ref, o_ref, st_s, p_s, *, tq, tk):
    grp = pl.program_id(1)
    qi = pl.program_id(2)
    hd = FOX_HEAD_DIM
    nh = 2 * FOX_PAIRS
    row = lax.broadcasted_iota(jnp.int32, (LANES, tq), 0)
    wq = []
    for i in range(nh):
        head = nh * grp + i
        qt = qt_ref[(i // 2) * LANES:(i // 2 + 1) * LANES, :]
        qh = jnp.where((row < hd) if i % 2 == 0 else (row >= hd), qt, jnp.zeros_like(qt))
        aug = jnp.where(((row & (CAUG_GROUP - 1)) == head) & (row < 3 * CAUG_GROUP), 1.0, 0.0)
        wq.append(jnp.concatenate([qh, aug.astype(BF16)], axis=0))
    ones = jnp.ones((ONES_ROWS, tk), BF16)
    krow = lax.broadcasted_iota(jnp.int32, (tk, tq), 0)
    qcol = lax.broadcasted_iota(jnp.int32, (tk, tq), 1)

    def scores(j):
        start = pl.multiple_of(j * tk, tk)
        c = c_ref[pl.ds(start, tk), :]
        lhs = [jnp.concatenate([k_ref[pl.ds(start, tk), pp * LANES:(pp + 1) * LANES], c], axis=1)
               for pp in range(FOX_PAIRS)]
        bmax = []
        for i in range(nh):
            s = jnp.dot(lhs[i // 2], wq[i], preferred_element_type=F32)
            st_s[i] = s
            bmax.append(jnp.max(s, axis=0, keepdims=True))
        return tuple(bmax)

    def softmax(j, m, bmax, masked):
        corrs, ms = [], []
        for i in range(nh):
            s = st_s[i]
            if masked:
                s = jnp.where(krow <= qcol, s, NEG)
                m_new = jnp.maximum(m[i], jnp.max(s, axis=0, keepdims=True))
            else:
                m_new = jnp.maximum(m[i], bmax[i])
            p_s[i] = jnp.exp2(s - m_new).astype(BF16)
            corrs.append(jnp.exp2(m[i] - m_new))
            ms.append(m_new)
        return tuple(corrs), tuple(ms)

    def pv(j, corr, acc):
        out = []
        for i in range(nh):
            vv = jnp.concatenate([vt_ref[j, i * hd:(i + 1) * hd, :], ones], axis=0)
            out.append(acc[i] * corr[i] + jnp.dot(vv, p_s[i], preferred_element_type=F32))
        return tuple(out)

    def body(j, carry):
        corr, m, bmax, acc = carry
        acc = pv(jnp.where(j == 0, nfull, j - 1), corr, acc)
        corr, m = softmax(j, m, bmax, masked=False)
        bmax = scores(j + 1)
        return corr, m, bmax, acc

    nfull = qi
    rep = lambda v: (v,) * nh
    scores(nfull)
    corr, m = softmax(nfull, rep(jnp.full((1, tq), NEG, F32)), None, masked=True)
    init = (corr, m, scores(0), rep(jnp.zeros((hd + ONES_ROWS, tq), F32)))
    corr, m, bmax, acc = lax.fori_loop(0, nfull, body, init)
    acc = pv(jnp.maximum(nfull - 1, 0), corr, acc)
    ot = jnp.concatenate([a[0:hd] / a[hd:hd + 1] for a in acc], axis=0)
    o_ref[...] = ot.T.astype(o_ref.dtype)


def _fox_attention(qt, k, vt, caug):
    bsz, seq, _ = k.shape
    tq, tk = FOX_TQ, FOX_TK
    assert tq == tk, "one masked diagonal block per query block"
    wide = FOX_PAIRS * LANES
    nh = 2 * FOX_PAIRS
    return pl.pallas_call(
        functools.partial(_fox_kernel, tq=tq, tk=tk),
        grid=(bsz, FOX_WIDTH // wide, seq // tq),
        in_specs=[
            pl.BlockSpec((None, wide, tq), lambda b, p, i: (b, p, i)),
            pl.BlockSpec((None, seq, wide), lambda b, p, i: (b, 0, p)),
            pl.BlockSpec((None, seq, LANES), lambda b, p, i: (b, 0, 0)),
            pl.BlockSpec((None, seq // tk, wide, tk), lambda b, p, i: (b, 0, p, 0)),
        ],
        out_specs=pl.BlockSpec((None, tq, wide), lambda b, p, i: (b, i, p)),
        out_shape=jax.ShapeDtypeStruct((bsz, seq, FOX_WIDTH), BF16),
        scratch_shapes=[pltpu.VMEM((nh, tk, tq), F32), pltpu.VMEM((nh, tk, tq), BF16)],
        compiler_params=_params("parallel", "parallel", "arbitrary"),
        name="fox_attention",
    )(qt, k, caug, vt)


def _out_proj_kernel(*refs, final):
    if final:
        o_ref, gate_ref, h_ref, w_ref, gf_ref, out_ref = refs
    else:
        o_ref, gate_ref, h_ref, w_ref, out_ref = refs
    g = gate_ref[...].astype(F32)
    a = o_ref[...].astype(F32) * (g * (1.0 / (1.0 + jnp.exp(-g))))
    y = h_ref[...] + jnp.dot(a.astype(BF16), w_ref[...], preferred_element_type=F32)
    if final:
        y = _rms(y, gf_ref[...])
    out_ref[...] = y


def _out_proj(o, gate, h, w, final_gain=None):
    bsz, seq, dm = h.shape
    width = o.shape[-1]
    tm = ROW_TILE
    row_spec = lambda wd: pl.BlockSpec((None, tm, wd), lambda b, i: (b, i, 0))
    in_specs = [row_spec(width), row_spec(width), row_spec(dm),
                pl.BlockSpec((width, dm), lambda b, i: (0, 0))]
    args = [o, gate, h, w.astype(BF16)]
    if final_gain is not None:
        in_specs.append(pl.BlockSpec((1, dm), lambda b, i: (0, 0)))
        args.append(final_gain.reshape(1, dm).astype(F32))
    return pl.pallas_call(
        functools.partial(_out_proj_kernel, final=final_gain is not None),
        grid=(bsz, seq // tm),
        in_specs=in_specs,
        out_specs=row_spec(dm),
        out_shape=jax.ShapeDtypeStruct((bsz, seq, dm), F32),
        compiler_params=_params("parallel", "parallel"),
        name="out_proj_final" if final_gain is not None else "out_proj",
    )(*args)


def _t5_bucket(dist):
    d = np.asarray(dist)
    exact = REL_BUCKETS // 2
    large = exact + (np.log(np.maximum(d, 1) / exact) / np.log(REL_MAX_DIST / exact)
                     * (REL_BUCKETS - exact)).astype(np.int32)
    large = np.minimum(large, REL_BUCKETS - 1)
    return np.where(d < exact, d, large).astype(np.int32)


def _bucket_ids():
    qi = np.arange(BLK)[:, None]
    ki = np.arange(2 * BLK)[None, :]
    rel = qi + BLK - ki
    out = []
    for window, dilation in DIL_GROUPS:
        steps = window // dilation
        in_band = (rel >= 0) & (rel <= steps)
        b = _t5_bucket(np.clip(rel, 0, steps) * dilation)
        out.append(np.where(in_band, b, REL_BUCKETS))
    return np.stack(out).astype(np.int32)


def _bias_kernel(table_ref, bucket_ref, out_ref):
    gh = pl.program_id(0)
    bk = bucket_ref[gh // DIL_HEADS]
    acc = jnp.full(bk.shape, NEG, F32)
    for b in range(REL_BUCKETS):
        acc = jnp.where(bk == b, table_ref[b, gh], acc)
    col = lax.broadcasted_iota(jnp.int32, bk.shape, 1)
    out_ref[0] = acc
    out_ref[1] = jnp.where(col < BLK, NEG, acc)


def _bias_tables(rel_bias):
    ngh = N_GROUPS * DIL_HEADS
    return pl.pallas_call(
        _bias_kernel,
        grid=(ngh,),
        in_specs=[
            pl.BlockSpec(memory_space=pltpu.SMEM),
            pl.BlockSpec((N_GROUPS, BLK, 2 * BLK), lambda i: (0, 0, 0)),
        ],
        out_specs=pl.BlockSpec((None, 2, BLK, 2 * BLK), lambda i: (i, 0, 0, 0)),
        out_shape=jax.ShapeDtypeStruct((ngh, 2, BLK, 2 * BLK), F32),
        compiler_params=_params("arbitrary"),
        name="t5_bias_tables",
    )(rel_bias.astype(F32), jnp.asarray(_bucket_ids()))


def _dilated_kernel(*refs):
    q_refs = refs[0:3]
    kv_refs = [refs[3 + 4 * g:7 + 4 * g] for g in range(N_GROUPS)]
    bias_ref = refs[15]
    o_ref = refs[16]
    acc_s, m_s, l_s = refs[17:20]
    first = jnp.where(pl.program_id(2) == 0, 1, 0)

    def rows(ref, start, size, stride):
        if stride == 1:
            return ref[pl.ds(start, size), :]
        return ref[pl.ds(start, size, stride=stride), :]

    for g, (window, dil) in enumerate(DIL_GROUPS):
        span = BLK * dil
        kprev_ref, kcur_ref, vprev_ref, vcur_ref = kv_refs[g]
        for j in range(DIL_BLOCK // span):
            def unit(r, _, g=g, dil=dil, span=span, j=j, kprev_ref=kprev_ref, kcur_ref=kcur_ref,
                     vprev_ref=vprev_ref, vcur_ref=vcur_ref):
                start = j * span + r
                q = rows(q_refs[g], start, BLK, dil).astype(BF16)
                if j == 0:
                    k = jnp.concatenate([rows(kprev_ref, r, BLK, dil), rows(kcur_ref, r, BLK, dil)], axis=0)
                    v = jnp.concatenate([rows(vprev_ref, r, BLK, dil), rows(vcur_ref, r, BLK, dil)], axis=0)
                    bias = bias_ref[g, first]
                else:
                    k = rows(kcur_ref, (j - 1) * span + r, 2 * BLK, dil)
                    v = rows(vcur_ref, (j - 1) * span + r, 2 * BLK, dil)
                    bias = bias_ref[g, 0]
                s = lax.dot_general(q, k.astype(BF16), _NT, preferred_element_type=F32) + bias
                m = jnp.max(s, axis=-1, keepdims=True)
                p = jnp.exp(s - m)
                l = jnp.sum(p, axis=-1, keepdims=True)
                acc = jnp.dot(p.astype(BF16), v.astype(BF16), preferred_element_type=F32)
                if dil == 1:
                    dst = pl.ds(start, BLK)
                else:
                    dst = pl.ds(start, BLK, stride=dil)
                acc_s[g, dst, :] = acc
                m_s[g, dst, :] = jnp.broadcast_to(m, (BLK, LANES))
                l_s[g, dst, :] = jnp.broadcast_to(l, (BLK, LANES))
                return 0

            for r in range(dil):
                unit(r, 0)

    chunk = 256

    def merge(i, _):
        sl = pl.ds(pl.multiple_of(i * chunk, chunk), chunk)
        ms = [m_s[g, sl, :] for g in range(N_GROUPS)]
        m_all = jnp.maximum(jnp.maximum(ms[0], ms[1]), ms[2])
        num = jnp.zeros((chunk, LANES), F32)
        den = jnp.zeros((chunk, LANES), F32)
        for g in range(N_GROUPS):
            w = jnp.exp(ms[g] - m_all)
            num = num + w * acc_s[g, sl, :]
            den = den + w * l_s[g, sl, :]
        o_ref[sl, :] = (num / den).astype(o_ref.dtype)
        return 0

    lax.fori_loop(0, DIL_BLOCK // chunk, merge, 0)


def _dilated_attention(uq, kv, bias):
    bsz, seq, _ = uq.shape
    nblk = seq // DIL_BLOCK
    in_specs = []
    args = []
    for g in range(N_GROUPS):
        in_specs.append(pl.BlockSpec((None, DIL_BLOCK, LANES),
                                     lambda b, h, n, g=g: (b, n, g * DIL_HEADS + h)))
        args.append(uq)
    for g, (_, dil) in enumerate(DIL_GROUPS):
        span = BLK * dil
        per = DIL_BLOCK // span
        for t in range(2):
            lane_blk = (2 * g + t) * DIL_HEADS
            in_specs.append(pl.BlockSpec(
                (None, span, LANES),
                lambda b, h, n, per=per, lane_blk=lane_blk: (b, jnp.maximum(n * per - 1, 0), lane_blk + h)))
            in_specs.append(pl.BlockSpec(
                (None, DIL_BLOCK, LANES), lambda b, h, n, lane_blk=lane_blk: (b, n, lane_blk + h)))
            args += [kv, kv]
    bias5 = bias.reshape(N_GROUPS, DIL_HEADS, 2, BLK, 2 * BLK)
    in_specs.append(pl.BlockSpec((N_GROUPS, None, 2, BLK, 2 * BLK), lambda b, h, n: (0, h, 0, 0, 0)))
    args.append(bias5)
    state = pltpu.VMEM((N_GROUPS, DIL_BLOCK, LANES), F32)
    return pl.pallas_call(
        _dilated_kernel,
        grid=(bsz, DIL_HEADS, nblk),
        in_specs=in_specs,
        out_specs=pl.BlockSpec((None, DIL_BLOCK, LANES), lambda b, h, n: (b, n, h)),
        out_shape=jax.ShapeDtypeStruct((bsz, seq, DIL_WIDTH), BF16),
        scratch_shapes=[state, state, state],
        compiler_params=_params("parallel", "parallel", "arbitrary"),
        name="dilated_attention",
    )(*args)


def kernel(x, rel_bias, norm_a, w_in_a, b_f_a, w_out_a, norm_kv, w_kv, norm_b, w_in_b, w_out_b, norm_f):
    bsz, seq, dm = x.shape
    assert dm == D_MODEL and seq % DIL_BLOCK == 0 and seq % ROW_TILE == 0
    n_a = w_in_a.shape[0]
    n_b = w_in_b.shape[0]
    h = x
    for i in range(n_a):
        k, gate, qt, vt, caug = _fox_proj(h, norm_a[i], w_in_a[i], b_f_a[i])
        o = _fox_attention(qt, k, vt, caug)
        h = _out_proj(o, gate, h, w_out_a[i])

    (kv,) = _norm_proj(h, norm_kv, w_kv, ((0, w_kv.shape[1], 1.0, F32),))
    bias = _bias_tables(rel_bias)
    gw = N_GROUPS * DIL_WIDTH
    for i in range(n_b):
        segs = ((0, gw, DIL_HEAD_DIM ** -0.5, F32), (gw, DIL_WIDTH, 1.0, BF16))
        uq, gate = _norm_proj(h, norm_b[i], w_in_b[i], segs)
        o = _dilated_attention(uq, kv, bias)
        h = _out_proj(o, gate, h, w_out_b[i], final_gain=norm_f if i == n_b - 1 else None)
    return h
```

```python
import functools

import jax
import jax.numpy as jnp
import numpy as np
from jax import lax
from jax.experimental import pallas as pl
from jax.experimental.pallas import tpu as pltpu

D_MODEL = 1024
BLK = 128
FOX_HEADS = 16
FOX_HEAD_DIM = D_MODEL // FOX_HEADS
FOX_WIDTH = FOX_HEADS * FOX_HEAD_DIM
DIL_GROUPS = ((128, 1), (512, 4), (2048, 16))
N_GROUPS = len(DIL_GROUPS)
DIL_HEADS = 8
DIL_HEAD_DIM = D_MODEL // DIL_HEADS
DIL_WIDTH = DIL_HEADS * DIL_HEAD_DIM
REL_BUCKETS = 32
REL_MAX_DIST = 2048
RMS_EPS = 1e-6
NEG = -1e30
LOG2E = 1.4426950408889634

LANES = 128
VMEM_LIMIT = 56 * 1024 * 1024
ROW_TILE = 512
COL_CHUNK = 1024
FOX_TK = ROW_TILE
FOX_TQ = 512
FOX_PAIRS = 2
ONES_ROWS = 16
CAUG_GROUP = FOX_HEADS
DIL_BLOCK = BLK * max(d for _, d in DIL_GROUPS)

F32 = jnp.float32
BF16 = jnp.bfloat16
_NT = (((1,), (1,)), ((), ()))


def _params(*sem):
    return pltpu.CompilerParams(dimension_semantics=sem, vmem_limit_bytes=VMEM_LIMIT)


def _rms(x, g):
    return x * lax.rsqrt(jnp.mean(x * x, axis=-1, keepdims=True) + RMS_EPS) * g


def _split3(x):
    p1 = x.astype(BF16)
    r1 = x - p1.astype(F32)
    p2 = r1.astype(BF16)
    p3 = (r1 - p2.astype(F32)).astype(BF16)
    return p1, p2, p3


def _norm_proj_kernel(*refs, segs):
    x_ref, g_ref, w_ref = refs[:3]
    out_refs = refs[3:3 + len(segs)]
    xh = _rms(x_ref[...], g_ref[...]).astype(BF16)
    for (start, width, scale), o_ref in zip(segs, out_refs):
        for c in range(0, width, COL_CHUNK):
            cw = min(COL_CHUNK, width - c)
            y = jnp.dot(xh, w_ref[:, start + c:start + c + cw], preferred_element_type=F32)
            if scale != 1.0:
                y = y * scale
            o_ref[:, c:c + cw] = y.astype(o_ref.dtype)


def _norm_proj(h, g, w, segs):
    bsz, seq, dm = h.shape
    tm = ROW_TILE
    n_in = w.shape[1]
    kern_segs = tuple((s, wd, sc) for s, wd, sc, _ in segs)
    return pl.pallas_call(
        functools.partial(_norm_proj_kernel, segs=kern_segs),
        grid=(bsz, seq // tm),
        in_specs=[
            pl.BlockSpec((None, tm, dm), lambda b, i: (b, i, 0)),
            pl.BlockSpec((1, dm), lambda b, i: (0, 0)),
            pl.BlockSpec((dm, n_in), lambda b, i: (0, 0)),
        ],
        out_specs=[pl.BlockSpec((None, tm, wd), lambda b, i: (b, i, 0)) for _, wd, _, _ in segs],
        out_shape=[jax.ShapeDtypeStruct((bsz, seq, wd), dt) for _, wd, _, dt in segs],
        compiler_params=_params("parallel", "parallel"),
        name="norm_proj",
    )(h, g.reshape(1, dm).astype(F32), w.astype(BF16))


def _fox_proj_kernel(x_ref, g_ref, wn_ref, wt_ref, wf_ref, bf_ref,
                     k_ref, gate_ref, qt_ref, vt_ref, caug_ref, carry_ref):
    tm = x_ref.shape[0]
    w = FOX_WIDTH
    xn = _rms(x_ref[...], g_ref[...])
    xh = xn.astype(BF16)
    k_ref[...] = jnp.dot(xh, wn_ref[:, 0:w], preferred_element_type=F32).astype(BF16)
    gate_ref[...] = jnp.dot(xh, wn_ref[:, w:2 * w], preferred_element_type=F32).astype(BF16)
    rows = 256
    qscale = FOX_HEAD_DIM ** -0.5 * LOG2E
    for r in range(0, w, rows):
        y = lax.dot_general(wt_ref[r:r + rows, :], xh, _NT, preferred_element_type=F32)
        qt_ref[r:r + rows, :] = (y * qscale).astype(BF16)
    for r in range(0, w, rows):
        y = lax.dot_general(wt_ref[w + r:w + r + rows, :], xh, _NT, preferred_element_type=F32)
        vt_ref[r:r + rows, :] = y.astype(BF16)

    @pl.when(pl.program_id(1) == 0)
    def _():
        carry_ref[...] = jnp.zeros_like(carry_ref)

    xl = (xn - xh.astype(F32)).astype(BF16)
    fh = jnp.dot(xh, wf_ref[...], preferred_element_type=F32)
    f = (fh[:, 0:LANES] + jnp.dot(xl, wf_ref[:, 0:LANES], preferred_element_type=F32)
         + fh[:, LANES:2 * LANES]) + bf_ref[...]
    ls = jnp.minimum(f, 0.0) - jnp.log1p(jnp.exp(-jnp.abs(f)))
    row = lax.broadcasted_iota(jnp.int32, (tm, tm), 0)
    col = lax.broadcasted_iota(jnp.int32, (tm, tm), 1)
    tri = jnp.where(row >= col, 1.0, 0.0).astype(BF16)
    cp = jnp.dot(tri, jnp.concatenate(_split3(ls), axis=1), preferred_element_type=F32)
    c = (cp[:, 0:LANES] + cp[:, LANES:2 * LANES] + cp[:, 2 * LANES:3 * LANES]) + carry_ref[0:1, :]
    carry_ref[...] = jnp.broadcast_to(c[tm - 1:tm, :], carry_ref.shape)
    d1, d2, d3 = _split3(c * (-LOG2E))
    lane = lax.broadcasted_iota(jnp.int32, (tm, LANES), 1)
    caug_ref[...] = jnp.where(lane < CAUG_GROUP, d1,
                              jnp.where(lane < 2 * CAUG_GROUP, d2,
                                        jnp.where(lane < 3 * CAUG_GROUP, d3, jnp.zeros_like(d1))))


def _fox_proj(h, g, w_in, b_f):
    bsz, seq, dm = h.shape
    tm = ROW_TILE
    w = FOX_WIDTH
    wn = jnp.concatenate([w_in[:, w:2 * w], w_in[:, 3 * w:4 * w]], axis=1).astype(BF16)
    wt = jnp.concatenate([w_in[:, 0:w], w_in[:, 2 * w:3 * w]], axis=1).T.astype(BF16)
    w_f = w_in[:, 4 * w:].astype(F32)
    pad = jnp.zeros((dm, LANES - 3 * FOX_HEADS), F32)
    wf_rep = jnp.concatenate([w_f, w_f, w_f, pad], axis=1)
    wfh = wf_rep.astype(BF16)
    wf = jnp.concatenate([wfh, (wf_rep - wfh.astype(F32)).astype(BF16)], axis=1)
    bf = b_f.astype(F32)
    bf_rep = jnp.concatenate([bf, bf, bf, jnp.zeros((LANES - 3 * FOX_HEADS,), F32)]).reshape(1, LANES)
    const = lambda shape: pl.BlockSpec(shape, lambda b, i: (0,) * len(shape))
    return pl.pallas_call(
        _fox_proj_kernel,
        grid=(bsz, seq // tm),
        in_specs=[
            pl.BlockSpec((None, tm, dm), lambda b, i: (b, i, 0)),
            const((1, dm)), const((dm, 2 * w)), const((2 * w, dm)),
            const((dm, 2 * LANES)), const((1, LANES)),
        ],
        out_specs=[
            pl.BlockSpec((None, tm, w), lambda b, i: (b, i, 0)),
            pl.BlockSpec((None, tm, w), lambda b, i: (b, i, 0)),
            pl.BlockSpec((None, w, tm), lambda b, i: (b, 0, i)),
            pl.BlockSpec((None, None, w, tm), lambda b, i: (b, i, 0, 0)),
            pl.BlockSpec((None, tm, LANES), lambda b, i: (b, i, 0)),
        ],
        out_shape=[
            jax.ShapeDtypeStruct((bsz, seq, w), BF16),
            jax.ShapeDtypeStruct((bsz, seq, w), BF16),
            jax.ShapeDtypeStruct((bsz, w, seq), BF16),
            jax.ShapeDtypeStruct((bsz, seq // tm, w, tm), BF16),
            jax.ShapeDtypeStruct((bsz, seq, LANES), BF16),
        ],
        scratch_shapes=[pltpu.VMEM((8, LANES), F32)],
        compiler_params=_params("parallel", "arbitrary"),
        name="fox_proj",
    )(h, g.reshape(1, dm).astype(F32), wn, wt, wf, bf_rep)


def _fox_kernel(qt_ref, k_ref, c_ref, vt_ref, o_ref, st_s, p_s, *, tq, tk):
    grp = pl.program_id(1)
    qi = pl.program_id(2)
    hd = FOX_HEAD_DIM
    nh = 2 * FOX_PAIRS
    row = lax.broadcasted_iota(jnp.int32, (LANES, tq), 0)
    wq = []
    for i in range(nh):
        head = nh * grp + i
        qt = qt_ref[(i // 2) * LANES:(i // 2 + 1) * LANES, :]
        qh = jnp.where((row < hd) if i % 2 == 0 else (row >= hd), qt, jnp.zeros_like(qt))
        aug = jnp.where(((row & (CAUG_GROUP - 1)) == head) & (row < 3 * CAUG_GROUP), 1.0, 0.0)
        wq.append(jnp.concatenate([qh, aug.astype(BF16)], axis=0))
    ones = jnp.ones((ONES_ROWS, tk), BF16)
    krow = lax.broadcasted_iota(jnp.int32, (tk, tq), 0)
    qcol = lax.broadcasted_iota(jnp.int32, (tk, tq), 1)

    def scores(j):
        start = pl.multiple_of(j * tk, tk)
        c = c_ref[pl.ds(start, tk), :]
        lhs = [jnp.concatenate([k_ref[pl.ds(start, tk), pp * LANES:(pp + 1) * LANES], c], axis=1)
               for pp in range(FOX_PAIRS)]
        bmax = []
        for i in range(nh):
            s = jnp.dot(lhs[i // 2], wq[i], preferred_element_type=F32)
            st_s[i] = s
            bmax.append(jnp.max(s, axis=0, keepdims=True))
        return tuple(bmax)

    def softmax(j, m, bmax, masked):
        corrs, ms = [], []
        for i in range(nh):
            s = st_s[i]
            if masked:
                s = jnp.where(krow <= qcol, s, NEG)
                m_new = jnp.maximum(m[i], jnp.max(s, axis=0, keepdims=True))
            else:
                m_new = jnp.maximum(m[i], bmax[i])
            p_s[i] = jnp.exp2(s - m_new).astype(BF16)
            corrs.append(jnp.exp2(m[i] - m_new))
            ms.append(m_new)
        return tuple(corrs), tuple(ms)

    def pv(j, corr, acc):
        out = []
        for i in range(nh):
            vv = jnp.concatenate([vt_ref[j, i * hd:(i + 1) * hd, :], ones], axis=0)
            out.append(acc[i] * corr[i] + jnp.dot(vv, p_s[i], preferred_element_type=F32))
        return tuple(out)

    def body(j, carry, more=True):
        corr, m, bmax, acc = carry
        acc = pv(jnp.where(j == 0, nfull, j - 1), corr, acc)
        corr, m = softmax(j, m, bmax, masked=False)
        if more:
            bmax = scores(j + 1)
        return corr, m, bmax, acc

    nfull = qi
    rep = lambda v: (v,) * nh
    scores(nfull)
    corr, m = softmax(nfull, rep(jnp.full((1, tq), NEG, F32)), None, masked=True)
    init = (corr, m, scores(0), rep(jnp.zeros((hd + ONES_ROWS, tq), F32)))
    carry = lax.fori_loop(0, nfull - 1, body, init)
    carry = lax.fori_loop(jnp.maximum(nfull - 1, 0), nfull, functools.partial(body, more=False), carry)
    corr, m, bmax, acc = carry
    acc = pv(jnp.maximum(nfull - 1, 0), corr, acc)
    ot = jnp.concatenate([a[0:hd] / a[hd:hd + 1] for a in acc], axis=0)
    o_ref[...] = ot.T.astype(o_ref.dtype)


def _fox_attention(qt, k, vt, caug):
    bsz, seq, _ = k.shape
    tq, tk = FOX_TQ, FOX_TK
    assert tq == tk, "one masked diagonal block per query block"
    wide = FOX_PAIRS * LANES
    nh = 2 * FOX_PAIRS
    return pl.pallas_call(
        functools.partial(_fox_kernel, tq=tq, tk=tk),
        grid=(bsz, FOX_WIDTH // wide, seq // tq),
        in_specs=[
            pl.BlockSpec((None, wide, tq), lambda b, p, i: (b, p, i)),
            pl.BlockSpec((None, seq, wide), lambda b, p, i: (b, 0, p)),
            pl.BlockSpec((None, seq, LANES), lambda b, p, i: (b, 0, 0)),
            pl.BlockSpec((None, seq // tk, wide, tk), lambda b, p, i: (b, 0, p, 0)),
        ],
        out_specs=pl.BlockSpec((None, tq, wide), lambda b, p, i: (b, i, p)),
        out_shape=jax.ShapeDtypeStruct((bsz, seq, FOX_WIDTH), BF16),
        scratch_shapes=[pltpu.VMEM((nh, tk, tq), F32), pltpu.VMEM((nh, tk, tq), BF16)],
        compiler_params=_params("parallel", "parallel", "arbitrary"),
        name="fox_attention",
    )(qt, k, caug, vt)


def _out_proj_kernel(*refs, final):
    if final:
        o_ref, gate_ref, h_ref, w_ref, gf_ref, out_ref = refs
    else:
        o_ref, gate_ref, h_ref, w_ref, out_ref = refs
    g = gate_ref[...].astype(F32)
    a = o_ref[...].astype(F32) * (g * (1.0 / (1.0 + jnp.exp(-g))))
    y = h_ref[...] + jnp.dot(a.astype(BF16), w_ref[...], preferred_element_type=F32)
    if final:
        y = _rms(y, gf_ref[...])
    out_ref[...] = y


def _out_proj(o, gate, h, w, final_gain=None):
    bsz, seq, dm = h.shape
    width = o.shape[-1]
    tm = ROW_TILE
    row_spec = lambda wd: pl.BlockSpec((None, tm, wd), lambda b, i: (b, i, 0))
    in_specs = [row_spec(width), row_spec(width), row_spec(dm),
                pl.BlockSpec((width, dm), lambda b, i: (0, 0))]
    args = [o, gate, h, w.astype(BF16)]
    if final_gain is not None:
        in_specs.append(pl.BlockSpec((1, dm), lambda b, i: (0, 0)))
        args.append(final_gain.reshape(1, dm).astype(F32))
    return pl.pallas_call(
        functools.partial(_out_proj_kernel, final=final_gain is not None),
        grid=(bsz, seq // tm),
        in_specs=in_specs,
        out_specs=row_spec(dm),
        out_shape=jax.ShapeDtypeStruct((bsz, seq, dm), F32),
        compiler_params=_params("parallel", "parallel"),
        name="out_proj_final" if final_gain is not None else "out_proj",
    )(*args)


def _t5_bucket(dist):
    d = np.asarray(dist)
    exact = REL_BUCKETS // 2
    large = exact + (np.log(np.maximum(d, 1) / exact) / np.log(REL_MAX_DIST / exact)
                     * (REL_BUCKETS - exact)).astype(np.int32)
    large = np.minimum(large, REL_BUCKETS - 1)
    return np.where(d < exact, d, large).astype(np.int32)


def _bucket_ids():
    qi = np.arange(BLK)[:, None]
    ki = np.arange(2 * BLK)[None, :]
    rel = qi + BLK - ki
    out = []
    for window, dilation in DIL_GROUPS:
        steps = window // dilation
        in_band = (rel >= 0) & (rel <= steps)
        b = _t5_bucket(np.clip(rel, 0, steps) * dilation)
        out.append(np.where(in_band, b, REL_BUCKETS))
    return np.stack(out).astype(np.int32)


def _bias_kernel(table_ref, bucket_ref, out_ref):
    gh = pl.program_id(0)
    bk = bucket_ref[gh // DIL_HEADS]
    acc = jnp.full(bk.shape, NEG, F32)
    for b in range(REL_BUCKETS):
        acc = jnp.where(bk == b, table_ref[b, gh], acc)
    col = lax.broadcasted_iota(jnp.int32, bk.shape, 1)
    out_ref[0] = acc
    out_ref[1] = jnp.where(col < BLK, NEG, acc)


def _bias_tables(rel_bias):
    ngh = N_GROUPS * DIL_HEADS
    return pl.pallas_call(
        _bias_kernel,
        grid=(ngh,),
        in_specs=[
            pl.BlockSpec(memory_space=pltpu.SMEM),
            pl.BlockSpec((N_GROUPS, BLK, 2 * BLK), lambda i: (0, 0, 0)),
        ],
        out_specs=pl.BlockSpec((None, 2, BLK, 2 * BLK), lambda i: (i, 0, 0, 0)),
        out_shape=jax.ShapeDtypeStruct((ngh, 2, BLK, 2 * BLK), F32),
        compiler_params=_params("arbitrary"),
        name="t5_bias_tables",
    )(rel_bias.astype(F32), jnp.asarray(_bucket_ids()))


def _dilated_kernel(*refs):
    q_refs = refs[0:3]
    kv_refs = [refs[3 + 4 * g:7 + 4 * g] for g in range(N_GROUPS)]
    bias_ref = refs[15]
    o_ref = refs[16]
    acc_s, m_s, l_s = refs[17:20]
    first = jnp.where(pl.program_id(2) == 0, 1, 0)

    def rows(ref, start, size, stride):
        if stride == 1:
            return ref[pl.ds(start, size), :]
        return ref[pl.ds(start, size, stride=stride), :]

    for g, (window, dil) in enumerate(DIL_GROUPS):
        span = BLK * dil
        kprev_ref, kcur_ref, vprev_ref, vcur_ref = kv_refs[g]
        for j in range(DIL_BLOCK // span):
            def unit(r, _, g=g, dil=dil, span=span, j=j, kprev_ref=kprev_ref, kcur_ref=kcur_ref,
                     vprev_ref=vprev_ref, vcur_ref=vcur_ref):
                start = j * span + r
                q = rows(q_refs[g], start, BLK, dil).astype(BF16)
                if j == 0:
                    k = jnp.concatenate([rows(kprev_ref, r, BLK, dil), rows(kcur_ref, r, BLK, dil)], axis=0)
                    v = jnp.concatenate([rows(vprev_ref, r, BLK, dil), rows(vcur_ref, r, BLK, dil)], axis=0)
                    bias = bias_ref[g, first]
                else:
                    k = rows(kcur_ref, (j - 1) * span + r, 2 * BLK, dil)
                    v = rows(vcur_ref, (j - 1) * span + r, 2 * BLK, dil)
                    bias = bias_ref[g, 0]
                s = lax.dot_general(q, k.astype(BF16), _NT, preferred_element_type=F32) + bias
                m = jnp.max(s, axis=-1, keepdims=True)
                p = jnp.exp(s - m)
                l = jnp.sum(p, axis=-1, keepdims=True)
                acc = jnp.dot(p.astype(BF16), v.astype(BF16), preferred_element_type=F32)
                if dil == 1:
                    dst = pl.ds(start, BLK)
                else:
                    dst = pl.ds(start, BLK, stride=dil)
                acc_s[g, dst, :] = acc
                m_s[g, dst, :] = jnp.broadcast_to(m, (BLK, LANES))
                l_s[g, dst, :] = jnp.broadcast_to(l, (BLK, LANES))
                return 0

            for r in range(dil):
                unit(r, 0)

    chunk = 256

    def merge(i, _):
        sl = pl.ds(pl.multiple_of(i * chunk, chunk), chunk)
        ms = [m_s[g, sl, :] for g in range(N_GROUPS)]
        m_all = jnp.maximum(jnp.maximum(ms[0], ms[1]), ms[2])
        num = jnp.zeros((chunk, LANES), F32)
        den = jnp.zeros((chunk, LANES), F32)
        for g in range(N_GROUPS):
            w = jnp.exp(ms[g] - m_all)
            num = num + w * acc_s[g, sl, :]
            den = den + w * l_s[g, sl, :]
        o_ref[sl, :] = (num / den).astype(o_ref.dtype)
        return 0

    lax.fori_loop(0, DIL_BLOCK // chunk, merge, 0)


def _dilated_attention(uq, kv, bias):
    bsz, seq, _ = uq.shape
    nblk = seq // DIL_BLOCK
    in_specs = []
    args = []
    for g in range(N_GROUPS):
        in_specs.append(pl.BlockSpec((None, DIL_BLOCK, LANES),
                                     lambda b, h, n, g=g: (b, n, g * DIL_HEADS + h)))
        args.append(uq)
    for g, (_, dil) in enumerate(DIL_GROUPS):
        span = BLK * dil
        per = DIL_BLOCK // span
        for t in range(2):
            lane_blk = (2 * g + t) * DIL_HEADS
            in_specs.append(pl.BlockSpec(
                (None, span, LANES),
                lambda b, h, n, per=per, lane_blk=lane_blk: (b, jnp.maximum(n * per - 1, 0), lane_blk + h)))
            in_specs.append(pl.BlockSpec(
                (None, DIL_BLOCK, LANES), lambda b, h, n, lane_blk=lane_blk: (b, n, lane_blk + h)))
            args += [kv, kv]
    bias5 = bias.reshape(N_GROUPS, DIL_HEADS, 2, BLK, 2 * BLK)
    in_specs.append(pl.BlockSpec((N_GROUPS, None, 2, BLK, 2 * BLK), lambda b, h, n: (0, h, 0, 0, 0)))
    args.append(bias5)
    state = pltpu.VMEM((N_GROUPS, DIL_BLOCK, LANES), F32)
    return pl.pallas_call(
        _dilated_kernel,
        grid=(bsz, DIL_HEADS, nblk),
        in_specs=in_specs,
        out_specs=pl.BlockSpec((None, DIL_BLOCK, LANES), lambda b, h, n: (b, n, h)),
        out_shape=jax.ShapeDtypeStruct((bsz, seq, DIL_WIDTH), BF16),
        scratch_shapes=[state, state, state],
        compiler_params=_params("parallel", "parallel", "arbitrary"),
        name="dilated_attention",
    )(*args)


def kernel(x, rel_bias, norm_a, w_in_a, b_f_a, w_out_a, norm_kv, w_kv, norm_b, w_in_b, w_out_b, norm_f):
    bsz, seq, dm = x.shape
    assert dm == D_MODEL and seq % DIL_BLOCK == 0 and seq % ROW_TILE == 0
    n_a = w_in_a.shape[0]
    n_b = w_in_b.shape[0]
    h = x
    for i in range(n_a):
        k, gate, qt, vt, caug = _fox_proj(h, norm_a[i], w_in_a[i], b_f_a[i])
        o = _fox_attention(qt, k, vt, caug)
        h = _out_proj(o, gate, h, w_out_a[i])

    (kv,) = _norm_proj(h, norm_kv, w_kv, ((0, w_kv.shape[1], 1.0, F32),))
    bias = _bias_tables(rel_bias)
    gw = N_GROUPS * DIL_WIDTH
    for i in range(n_b):
        segs = ((0, gw, DIL_HEAD_DIM ** -0.5, F32), (gw, DIL_WIDTH, 1.0, BF16))
        uq, gate = _norm_proj(h, norm_b[i], w_in_b[i], segs)
        o = _dilated_attention(uq, kv, bias)
        h = _out_proj(o, gate, h, w_out_b[i], final_gain=norm_f if i == n_b - 1 else None)
    return h
```
